```python
import jax, jax.numpy as jnp
from jax import lax
import numpy as np

D_MODEL = 1024
BATCH = 32
SEQ = 2048
DEPTH = 1

SSD_EXPAND = 2
D_INNER = SSD_EXPAND * D_MODEL
SSD_HEAD_DIM = 64
SSD_HEADS = D_INNER // SSD_HEAD_DIM
SSD_GROUPS = 4
SSD_HEADS_PER_GROUP = SSD_HEADS // SSD_GROUPS
SSD_STATE = 128
SSD_CONV = 4
SSD_CHUNK = 128
SSD_CONV_DIM = D_INNER + 2 * SSD_GROUPS * SSD_STATE
MLA_HEADS = 16
MLA_Q_LORA = 256
MLA_KV_LORA = 128
MLA_NOPE = 64
MLA_ROPE = 32
MLA_QK_DIM = MLA_NOPE + MLA_ROPE
MLA_V_DIM = 64
MLA_WIDTH = MLA_HEADS * MLA_V_DIM
ROPE_THETA = 10000.0
ATTN_Q_BLOCK = 128
D_FF = 2816
FFN_CONV = 3
NORM_EPS = 1e-6

OFF_Z = D_INNER
OFF_XBC = OFF_Z + SSD_CONV_DIM
OFF_DT = OFF_XBC + SSD_HEADS
OFF_QA = OFF_DT + MLA_Q_LORA
OFF_KVA = OFF_QA + MLA_KV_LORA
OFF_KR = OFF_KVA + MLA_ROPE
IN_COLS = OFF_KR + 2 * D_MODEL

kernel_name = "hybrid_ssd_mla_gated_convffn"


def rms_norm(x, g):
    xf = x.astype(jnp.float32)
    xf = xf * lax.rsqrt(jnp.mean(xf * xf, axis=-1, keepdims=True) + NORM_EPS)
    return (xf * g.astype(jnp.float32)).astype(x.dtype)


def causal_depthwise_conv(x, w, b):
    k, c = w.shape
    y = lax.conv_general_dilated(x, w[:, None, :].astype(x.dtype), window_strides=(1,),
                                 padding=[(k - 1, 0)], dimension_numbers=('NWC', 'WIO', 'NWC'),
                                 feature_group_count=c)
    return y + b


def rope_tables(seq, dim):
    inv = 1.0 / (ROPE_THETA ** (jnp.arange(0, dim, 2, dtype=jnp.float32) / dim))
    ang = jnp.arange(seq, dtype=jnp.float32)[:, None] * inv[None, :]
    return jnp.cos(ang)[None, :, None, :], jnp.sin(ang)[None, :, None, :]


def apply_rope(x, cos, sin):
    x1, x2 = jnp.split(x.astype(jnp.float32), 2, axis=-1)
    return jnp.concatenate([x1 * cos - x2 * sin, x1 * sin + x2 * cos], axis=-1).astype(x.dtype)


def ssd_chunked_scan(xdt, a, bm, cm):
    b, l, h, p = xdt.shape
    nc = l // SSD_CHUNK
    f32 = jnp.float32
    xc = xdt.astype(f32).reshape(b, nc, SSD_CHUNK, SSD_GROUPS, SSD_HEADS_PER_GROUP, p)
    bc = bm.astype(f32).reshape(b, nc, SSD_CHUNK, SSD_GROUPS, SSD_STATE)
    cc = cm.astype(f32).reshape(b, nc, SSD_CHUNK, SSD_GROUPS, SSD_STATE)
    ac = a.astype(f32).reshape(b, nc, SSD_CHUNK, SSD_GROUPS, SSD_HEADS_PER_GROUP).transpose(0, 1, 3, 4, 2)
    a_cum = jnp.cumsum(ac, axis=-1)
    causal = jnp.tril(jnp.ones((SSD_CHUNK, SSD_CHUNK), dtype=bool))
    seg = a_cum[..., :, None] - a_cum[..., None, :]
    decay_in = jnp.exp(jnp.where(causal, seg, -jnp.inf))
    scores = jnp.einsum('bclgn,bcsgn->bcgls', cc, bc)
    y_diag = jnp.einsum('bcgls,bcgels,bcsgep->bclgep', scores, decay_in, xc)
    decay_to_end = jnp.exp(a_cum[..., -1:] - a_cum)
    chunk_states = jnp.einsum('bclgn,bcgel,bclgep->bcgepn', bc, decay_to_end, xc)
    chunk_decay = jnp.exp(a_cum[..., -1])

    def carry_state(state, inp):
        new_states, decay = inp
        return state * decay[..., None, None] + new_states, state

    init = jnp.zeros_like(chunk_states[:, 0])
    _, prev = lax.scan(carry_state, init,
                       (jnp.moveaxis(chunk_states, 1, 0), jnp.moveaxis(chunk_decay, 1, 0)))
    y_off = jnp.einsum('bclgn,cbgepn,bcgel->bclgep', cc, prev, jnp.exp(a_cum))
    return (y_diag + y_off).reshape(b, l, h, p).astype(xdt.dtype)


def ssd_mixer(z, xbc, dt_raw, conv_w, conv_b, dt_bias, a_log, d_skip, norm_g):
    b, l, _ = z.shape
    xbc = jax.nn.silu(causal_depthwise_conv(xbc, conv_w, conv_b))
    xs, bm, cm = jnp.split(xbc, [D_INNER, D_INNER + SSD_GROUPS * SSD_STATE], axis=-1)
    xs = xs.reshape(b, l, SSD_HEADS, SSD_HEAD_DIM)
    bm = bm.reshape(b, l, SSD_GROUPS, SSD_STATE)
    cm = cm.reshape(b, l, SSD_GROUPS, SSD_STATE)
    dt = jax.nn.softplus(dt_raw.astype(jnp.float32) + dt_bias.astype(jnp.float32))
    a = -jnp.exp(a_log.astype(jnp.float32))
    y = ssd_chunked_scan(xs * dt[..., None].astype(xs.dtype), dt * a, bm, cm)
    y = y + xs * d_skip[:, None]
    y = y.reshape(b, l, D_INNER) * jax.nn.silu(z)
    y = rms_norm(y.reshape(b, l, SSD_GROUPS, D_INNER // SSD_GROUPS),
                 norm_g.reshape(SSD_GROUPS, D_INNER // SSD_GROUPS))
    return y.reshape(b, l, D_INNER)


def blocked_causal_attention(q, k, v):
    b, l, h, dk = q.shape
    scale = dk ** -0.5
    key_pos = jnp.arange(l)

    def one_block(i):
        start = i * ATTN_Q_BLOCK
        qb = lax.dynamic_slice_in_dim(q, start, ATTN_Q_BLOCK, axis=1)
        s = jnp.einsum('bqhd,bkhd->bhqk', qb, k).astype(jnp.float32) * scale
        qpos = start + jnp.arange(ATTN_Q_BLOCK)
        s = jnp.where(qpos[:, None] >= key_pos[None, :], s, -jnp.inf)
        p = jax.nn.softmax(s, axis=-1).astype(v.dtype)
        return jnp.einsum('bhqk,bkhd->bqhd', p, v)

    out = lax.map(one_block, jnp.arange(l // ATTN_Q_BLOCK))
    return jnp.moveaxis(out, 0, 1).reshape(b, l, h * v.shape[-1])


def mla_mixer(q_a, kv_a, k_rope_raw, q_a_norm_g, w_uq, kv_a_norm_g, w_ukv, q_norm_g, k_norm_g, cos, sin):
    b, l, _ = q_a.shape
    q = (rms_norm(q_a, q_a_norm_g) @ w_uq).reshape(b, l, MLA_HEADS, MLA_QK_DIM)
    kv = (rms_norm(kv_a, kv_a_norm_g) @ w_ukv).reshape(b, l, MLA_HEADS, MLA_NOPE + MLA_V_DIM)
    k_nope, v = jnp.split(kv, [MLA_NOPE], axis=-1)
    k_rope = jnp.broadcast_to(k_rope_raw[:, :, None, :], (b, l, MLA_HEADS, MLA_ROPE))
    k = jnp.concatenate([k_nope, k_rope], axis=-1)
    q = rms_norm(q, q_norm_g)
    k = rms_norm(k, k_norm_g)
    q = jnp.concatenate([q[..., :MLA_NOPE], apply_rope(q[..., MLA_NOPE:], cos, sin)], axis=-1)
    k = jnp.concatenate([k[..., :MLA_NOPE], apply_rope(k[..., MLA_NOPE:], cos, sin)], axis=-1)
    return blocked_causal_attention(q, k, v)


def setup_inputs(seed: int = 0) -> dict:
    key = jax.random.key(seed)
    ks = jax.random.split(key, 24)
    f32 = jnp.float32

    def dense(k, shape, fan_in):
        return jax.random.normal(k, (DEPTH,) + shape, f32) * fan_in ** -0.5

    def gain(k, n):
        return 1.0 + 0.02 * jax.random.normal(k, (DEPTH, n), f32)

    def bias(k, n):
        return 0.02 * jax.random.normal(k, (DEPTH, n), f32)

    dt0 = jnp.exp(jax.random.uniform(ks[5], (DEPTH, SSD_HEADS), f32, np.log(1e-3), np.log(1e-1)))
    return {
        "x": jax.random.normal(ks[0], (BATCH, SEQ, D_MODEL), f32),
        "norm_mix_g": gain(ks[1], D_MODEL),
        "w_in": dense(ks[2], (D_MODEL, IN_COLS), D_MODEL),
        "conv_ssd_w": dense(ks[3], (SSD_CONV, SSD_CONV_DIM), SSD_CONV),
        "conv_ssd_b": bias(ks[4], SSD_CONV_DIM),
        "dt_bias": dt0 + jnp.log(-jnp.expm1(-dt0)),
        "a_log": jnp.log(jax.random.uniform(ks[6], (DEPTH, SSD_HEADS), f32, 1.0, 16.0)),
        "d_skip": 1.0 + 0.1 * jax.random.normal(ks[7], (DEPTH, SSD_HEADS), f32),
        "ssd_norm_g": gain(ks[8], D_INNER),
        "w_ssd_proj": dense(ks[9], (D_INNER, D_MODEL), D_INNER),
        "q_a_norm_g": gain(ks[10], MLA_Q_LORA),
        "w_uq": dense(ks[11], (MLA_Q_LORA, MLA_HEADS * MLA_QK_DIM), MLA_Q_LORA),
        "kv_a_norm_g": gain(ks[12], MLA_KV_LORA),
        "w_ukv": dense(ks[13], (MLA_KV_LORA, MLA_HEADS * (MLA_NOPE + MLA_V_DIM)), MLA_KV_LORA),
        "q_norm_g": gain(ks[14], MLA_QK_DIM),
        "k_norm_g": gain(ks[15], MLA_QK_DIM),
        "w_mla_proj": dense(ks[16], (MLA_WIDTH, D_MODEL), MLA_WIDTH),
        "gate_b": bias(ks[17], 2 * D_MODEL),
        "w_o": dense(ks[18], (D_MODEL, D_MODEL), D_MODEL),
        "norm_ffn_g": gain(ks[19], D_MODEL),
        "w_up": dense(ks[20], (D_MODEL, 2 * D_FF), D_MODEL),
        "conv_ffn_w": dense(ks[21], (FFN_CONV, 2 * D_FF), FFN_CONV),
        "conv_ffn_b": bias(ks[22], 2 * D_FF),
        "w_down": dense(ks[23], (D_FF, D_MODEL), D_FF),
    }


def reference(x, norm_mix_g, w_in, conv_ssd_w, conv_ssd_b, dt_bias, a_log, d_skip, ssd_norm_g,
              w_ssd_proj, q_a_norm_g, w_uq, kv_a_norm_g, w_ukv, q_norm_g, k_norm_g, w_mla_proj,
              gate_b, w_o, norm_ffn_g, w_up, conv_ffn_w, conv_ffn_b, w_down):
    cos, sin = rope_tables(x.shape[1], MLA_ROPE)
    for i in range(DEPTH):
        h = rms_norm(x, norm_mix_g[i])
        proj = h @ w_in[i]
        z, xbc, dt_raw, q_a, kv_a, k_rope_raw, gates = jnp.split(
            proj, [OFF_Z, OFF_XBC, OFF_DT, OFF_QA, OFF_KVA, OFF_KR], axis=-1)
        y_ssd = ssd_mixer(z, xbc, dt_raw, conv_ssd_w[i], conv_ssd_b[i], dt_bias[i], a_log[i],
                          d_skip[i], ssd_norm_g[i]) @ w_ssd_proj[i]
        y_mla = mla_mixer(q_a, kv_a, k_rope_raw, q_a_norm_g[i], w_uq[i], kv_a_norm_g[i], w_ukv[i],
                          q_norm_g[i], k_norm_g[i], cos, sin) @ w_mla_proj[i]
        g_ssd, g_mla = jnp.split(jax.nn.sigmoid(gates + gate_b[i]), 2, axis=-1)
        x = x + (g_ssd * y_ssd + g_mla * y_mla) @ w_o[i]
        h = rms_norm(x, norm_ffn_g[i])
        u = causal_depthwise_conv(h @ w_up[i], conv_ffn_w[i], conv_ffn_b[i])
        u_gate, u_val = jnp.split(u, 2, axis=-1)
        x = x + (jax.nn.silu(u_gate) * u_val) @ w_down[i]
    return x
```

```python
import functools

import jax
import jax.numpy as jnp
from jax import lax
from jax.experimental import pallas as pl
from jax.experimental.pallas import tpu as pltpu

F32 = jnp.float32
BF16 = jnp.bfloat16

D_MODEL = 1024
D_INNER = 2048
SSD_HEADS = 32
SSD_HEAD_DIM = 64
SSD_GROUPS = 4
SSD_STATE = 128
SSD_CONV = 4
SSD_CHUNK = 128
SSD_CONV_DIM = D_INNER + 2 * SSD_GROUPS * SSD_STATE
GROUP_WIDTH = D_INNER // SSD_GROUPS
MLA_HEADS = 16
MLA_Q_LORA = 256
MLA_KV_LORA = 128
MLA_NOPE = 64
MLA_ROPE = 32
MLA_QK_DIM = MLA_NOPE + MLA_ROPE
MLA_QK_PAD = 128
MLA_V_DIM = 64
ROPE_THETA = 10000.0
D_FF = 2816
FFN_CONV = 3
NORM_EPS = 1e-6

OFF_Z = D_INNER
OFF_XBC = OFF_Z + SSD_CONV_DIM
OFF_DT = OFF_XBC + SSD_HEADS
OFF_QA = OFF_DT + MLA_Q_LORA
OFF_KVA = OFF_QA + MLA_KV_LORA
OFF_KR = OFF_KVA + MLA_ROPE

SUBLANES = 8
ATTN_TILE = 256
FFN_COL_CHUNK = 256
VMEM_LIMIT = 56 * 1024 * 1024


def _params(sem):
    return pltpu.CompilerParams(dimension_semantics=sem, vmem_limit_bytes=VMEM_LIMIT)


def _rms_rows(x, g):
    ms = jnp.mean(x * x, axis=-1, keepdims=True)
    return x * lax.rsqrt(ms + NORM_EPS) * g


def _rms_cols(x, g):
    ms = jnp.mean(x * x, axis=0, keepdims=True)
    return x * lax.rsqrt(ms + NORM_EPS) * g


def _sigmoid(x):
    return 1.0 / (1.0 + jnp.exp(-x))


def _silu(x):
    return x * _sigmoid(x)


def _norm_proj_kernel(x_ref, g_ref, w_ref, o_ref, h_ref):
    @pl.when(pl.program_id(1) == 0)
    def _():
        h_ref[...] = _rms_rows(x_ref[...], g_ref[...]).astype(BF16)

    o_ref[...] = jnp.dot(h_ref[...], w_ref[...], preferred_element_type=F32).astype(o_ref.dtype)


def _norm_proj(x2d, g, w, tm, tn):
    t, d = x2d.shape
    n = w.shape[1]
    return pl.pallas_call(
        _norm_proj_kernel,
        grid=(t // tm, n // tn),
        in_specs=[
            pl.BlockSpec((tm, d), lambda i, j: (i, 0)),
            pl.BlockSpec((1, d), lambda i, j: (0, 0)),
            pl.BlockSpec((d, tn), lambda i, j: (0, j)),
        ],
        out_specs=pl.BlockSpec((tm, tn), lambda i, j: (i, j)),
        out_shape=jax.ShapeDtypeStruct((t, n), BF16),
        scratch_shapes=[pltpu.VMEM((tm, d), BF16)],
        compiler_params=_params(("parallel", "arbitrary")),
        name="norm_proj",
    )(x2d, g, w)


def _rope_rows(x1, x2, cos, sin):
    return x1 * cos - x2 * sin, x1 * sin + x2 * cos


def _mla_prep_kernel(x_ref, g_ref, ws_ref, qag_ref, wuq_ref, kvg_ref, wukv_ref, qg_ref, kg_ref,
                     cos_ref, sin_ref, qt_ref, k_ref, vt_ref, dt_ref):
    tm = x_ref.shape[1]
    h = _rms_rows(x_ref[0], g_ref[...]).astype(BF16)
    st = lax.dot_general(ws_ref[...], h, (((1,), (1,)), ((), ())), preferred_element_type=F32)
    qa = st[0:MLA_Q_LORA]
    kva = st[MLA_Q_LORA:MLA_Q_LORA + MLA_KV_LORA]
    o = MLA_Q_LORA + MLA_KV_LORA
    dt_ref[0] = st[o:o + SSD_HEADS]
    kr = st[o + SSD_HEADS:o + SSD_HEADS + MLA_ROPE]

    qa_n = _rms_cols(qa, qag_ref[...]).astype(BF16)
    kva_n = _rms_cols(kva, kvg_ref[...]).astype(BF16)
    q_all = jnp.dot(wuq_ref[...], qa_n, preferred_element_type=F32)
    kv_all = jnp.dot(wukv_ref[...], kva_n, preferred_element_type=F32)

    cos = cos_ref[...]
    sin = sin_ref[...]
    qg = qg_ref[...]
    kg = kg_ref[...]
    half = MLA_ROPE // 2
    scale = MLA_QK_DIM ** -0.5
    kr_ss = jnp.sum(kr * kr, axis=0, keepdims=True)
    pad = jnp.zeros((MLA_QK_PAD - MLA_QK_DIM, tm), F32)
    n_sub = tm // ATTN_TILE
    for hd in range(MLA_HEADS):
        qh = _rms_cols(q_all[MLA_QK_DIM * hd:MLA_QK_DIM * (hd + 1)], qg) * scale
        r1, r2 = _rope_rows(qh[MLA_NOPE:MLA_NOPE + half], qh[MLA_NOPE + half:], cos, sin)
        qfull = jnp.concatenate([qh[:MLA_NOPE], r1, r2, pad], axis=0).astype(BF16)

        base = (MLA_NOPE + MLA_V_DIM) * hd
        kn = kv_all[base:base + MLA_NOPE]
        v = kv_all[base + MLA_NOPE:base + MLA_NOPE + MLA_V_DIM].astype(BF16)
        ss = (jnp.sum(kn * kn, axis=0, keepdims=True) + kr_ss) * (1.0 / MLA_QK_DIM)
        rs = lax.rsqrt(ss + NORM_EPS)
        kn = kn * rs * kg[:MLA_NOPE]
        krn = kr * rs * kg[MLA_NOPE:]
        r1, r2 = _rope_rows(krn[:half], krn[half:], cos, sin)
        kfull = jnp.concatenate([kn, r1, r2, pad], axis=0)
        k_ref[0, hd] = kfull.T.astype(BF16)
        for j in range(n_sub):
            sl = slice(j * ATTN_TILE, (j + 1) * ATTN_TILE)
            qt_ref[0, hd, j] = qfull[:, sl]
            vt_ref[0, hd, j] = v[:, sl]


def _mla_prep(x, g, ws_t, qag, wuq_t, kvg, wukv_t, qg, kg, cos_t, sin_t, tm):
    b, s, d = x.shape
    nt = s // ATTN_TILE
    n_sub = tm // ATTN_TILE
    const = lambda shape: pl.BlockSpec(shape, lambda i, j: (0,) * len(shape))
    return pl.pallas_call(
        _mla_prep_kernel,
        grid=(b, s // tm),
        in_specs=[
            pl.BlockSpec((1, tm, d), lambda i, j: (i, j, 0)),
            const(g.shape), const(ws_t.shape), const(qag.shape), const(wuq_t.shape),
            const(kvg.shape), const(wukv_t.shape), const(qg.shape), const(kg.shape),
            pl.BlockSpec((MLA_ROPE // 2, tm), lambda i, j: (0, j)),
            pl.BlockSpec((MLA_ROPE // 2, tm), lambda i, j: (0, j)),
        ],
        out_specs=[
            pl.BlockSpec((1, MLA_HEADS, n_sub, MLA_QK_PAD, ATTN_TILE), lambda i, j: (i, 0, j, 0, 0)),
            pl.BlockSpec((1, MLA_HEADS, tm, MLA_QK_PAD), lambda i, j: (i, 0, j, 0)),
            pl.BlockSpec((1, MLA_HEADS, n_sub, MLA_V_DIM, ATTN_TILE), lambda i, j: (i, 0, j, 0, 0)),
            pl.BlockSpec((1, SSD_HEADS, tm), lambda i, j: (i, 0, j)),
        ],
        out_shape=[
            jax.ShapeDtypeStruct((b, MLA_HEADS, nt, MLA_QK_PAD, ATTN_TILE), BF16),
            jax.ShapeDtypeStruct((b, MLA_HEADS, s, MLA_QK_PAD), BF16),
            jax.ShapeDtypeStruct((b, MLA_HEADS, nt, MLA_V_DIM, ATTN_TILE), BF16),
            jax.ShapeDtypeStruct((b, SSD_HEADS, s), F32),
        ],
        compiler_params=_params(("parallel", "parallel")),
        name="mla_prep",
    )(x, g, ws_t, qag, wuq_t, kvg, wukv_t, qg, kg, cos_t, sin_t)


def _split3_rows(v):
    hi = v.astype(BF16).astype(F32)
    r1 = v - hi
    mid = r1.astype(BF16).astype(F32)
    lo = (r1 - mid).astype(BF16).astype(F32)
    return jnp.concatenate([hi, mid, lo, jnp.zeros_like(v)], axis=0)


def _ssd_kernel(xbc_ref, z_ref, dt_ref, cw_ref, cb_ref, dtb_ref, alog_ref, dskip_ref, ng_ref, e3_ref,
                o_ref, xprev_ref, state_ref):
    L = SSD_CHUNK

    @pl.when(pl.program_id(1) == 0)
    def _():
        xprev_ref[...] = jnp.zeros_like(xprev_ref)
        state_ref[...] = jnp.zeros_like(state_ref)

    cur = xbc_ref[0].astype(F32)
    ext = jnp.concatenate([xprev_ref[...], cur], axis=0)
    xprev_ref[...] = cur[L - SUBLANES:]
    cw = cw_ref[...]
    acc = cb_ref[...] + ext[SUBLANES:] * cw[SSD_CONV - 1:SSD_CONV]
    for k in range(1, SSD_CONV):
        acc = acc + ext[SUBLANES - k:SUBLANES - k + L] * cw[SSD_CONV - 1 - k:SSD_CONV - k]
    act = _silu(acc)
    xs = act[:, :D_INNER]
    bm = act[:, D_INNER:D_INNER + SSD_GROUPS * SSD_STATE]
    cm = act[:, D_INNER + SSD_GROUPS * SSD_STATE:]

    dt_raw = dt_ref[0] + dtb_ref[...]
    dt_t = jnp.maximum(dt_raw, 0.0) + jnp.log(1.0 + jnp.exp(-jnp.abs(dt_raw)))
    a_t = dt_t * (-jnp.exp(alog_ref[...]))
    row = lax.broadcasted_iota(jnp.int32, (L, L), 0)
    col = lax.broadcasted_iota(jnp.int32, (L, L), 1)
    triu = (row <= col).astype(F32)
    acum_t = jnp.dot(a_t, triu, preferred_element_type=F32, precision=lax.Precision.HIGHEST)
    w2_t = dt_t * jnp.exp(acum_t[:, L - 1:L] - acum_t)
    ea_t = jnp.exp(acum_t)

    def expand(v_t):
        p = _split3_rows(v_t).T.astype(BF16)
        return jnp.dot(p, e3_ref[...], preferred_element_type=F32)

    dt_x = expand(dt_t)
    w2_x = expand(w2_t)
    ea_x = expand(ea_t)
    acum_c = jnp.concatenate([acum_t, jnp.zeros((L - SSD_HEADS, L), F32)], axis=0).T

    xdt = xs * dt_x
    xw2 = (xs * w2_x).astype(BF16)
    causal = row >= col
    left = col < SSD_HEAD_DIM
    heads_per_group = SSD_HEADS // SSD_GROUPS
    y_parts = []
    for g in range(SSD_GROUPS):
        gs = slice(g * GROUP_WIDTH, (g + 1) * GROUP_WIDTH)
        b_g = bm[:, g * SSD_STATE:(g + 1) * SSD_STATE]
        c_g = cm[:, g * SSD_STATE:(g + 1) * SSD_STATE].astype(BF16)
        scores = lax.dot_general(c_g, b_g.astype(BF16), (((1,), (1,)), ((), ())),
                                 preferred_element_type=F32)
        prev = state_ref[g]
        y_g = jnp.dot(c_g, prev.astype(BF16), preferred_element_type=F32) * ea_x[:, gs]
        diag = []
        for pair in range(heads_per_group // 2):
            ms = []
            for e in range(2):
                hd = g * heads_per_group + 2 * pair + e
                seg = acum_c[:, hd:hd + 1] - acum_t[hd:hd + 1, :]
                ms.append(scores * jnp.exp(jnp.where(causal, seg, -jnp.inf)))
            m2 = jnp.concatenate(ms, axis=1).astype(BF16)
            c0 = g * GROUP_WIDTH + pair * 2 * SSD_HEAD_DIM
            x2 = xdt[:, c0:c0 + 2 * SSD_HEAD_DIM]
            r2 = jnp.concatenate([jnp.where(left, x2, 0.0), jnp.where(left, 0.0, x2)], axis=0).astype(BF16)
            diag.append(jnp.dot(m2, r2, preferred_element_type=F32))
        y_parts.append(y_g + jnp.concatenate(diag, axis=1))
        new_state = jnp.dot(b_g.T.astype(BF16), xw2[:, gs], preferred_element_type=F32)
        state_ref[g] = prev * ea_x[L - 1:L, gs] + new_state

    y = jnp.concatenate(y_parts, axis=1) + xs * dskip_ref[...]
    y = y * _silu(z_ref[0].astype(F32))
    ng = ng_ref[...]
    outs = []
    for g in range(SSD_GROUPS):
        gs = slice(g * GROUP_WIDTH, (g + 1) * GROUP_WIDTH)
        outs.append(_rms_rows(y[:, gs], ng[:, gs]))
    o_ref[0] = jnp.concatenate(outs, axis=1).astype(o_ref.dtype)


def _ssd(xbc, z, dt_t, cw, cb, dtb, alog, dskip_x, ng, e3):
    b, s, _ = xbc.shape
    L = SSD_CHUNK
    const = lambda shape: pl.BlockSpec(shape, lambda i, j: (0,) * len(shape))
    return pl.pallas_call(
        _ssd_kernel,
        grid=(b, s // L),
        in_specs=[
            pl.BlockSpec((1, L, SSD_CONV_DIM), lambda i, j: (i, j, 0)),
            pl.BlockSpec((1, L, D_INNER), lambda i, j: (i, j, 0)),
            pl.BlockSpec((1, SSD_HEADS, L), lambda i, j: (i, 0, j)),
            const(cw.shape), const(cb.shape), const(dtb.shape), const(alog.shape),
            const(dskip_x.shape), const(ng.shape), const(e3.shape),
        ],
        out_specs=pl.BlockSpec((1, L, D_INNER), lambda i, j: (i, j, 0)),
        out_shape=jax.ShapeDtypeStruct((b, s, D_INNER), BF16),
        scratch_shapes=[
            pltpu.VMEM((SUBLANES, SSD_CONV_DIM), F32),
            pltpu.VMEM((SSD_GROUPS, SSD_STATE, GROUP_WIDTH), F32),
        ],
        compiler_params=_params(("parallel", "arbitrary")),
        name="ssd",
    )(xbc, z, dt_t, cw, cb, dtb, alog, dskip_x, ng, e3)


def _attn_kernel(qt_ref, k_ref, vt_ref, o_ref):
    nq = qt_ref.shape[2]
    heads = qt_ref.shape[1]
    T = ATTN_TILE
    row = lax.broadcasted_iota(jnp.int32, (T, T), 0)
    col = lax.broadcasted_iota(jnp.int32, (T, T), 1)
    diag_ok = row <= col

    def tile_update(h, qt, kj, carry, masked):
        m, l, acc = carry
        k0 = pl.multiple_of(kj * T, T)
        kt = k_ref[0, h, pl.ds(k0, T), :]
        s = jnp.dot(kt, qt, preferred_element_type=F32)
        if masked:
            s = jnp.where(diag_ok, s, -jnp.inf)
        m_new = jnp.maximum(m, jnp.max(s, axis=0, keepdims=True))
        p = jnp.exp(s - m_new)
        alpha = jnp.exp(m - m_new)
        l = alpha * l + jnp.sum(p, axis=0, keepdims=True)
        acc = alpha * acc + jnp.dot(vt_ref[0, h, kj], p.astype(BF16), preferred_element_type=F32)
        return m_new, l, acc

    def q_tile(qi, _):
        qts = [qt_ref[0, h, qi] for h in range(heads)]
        init = tuple((jnp.full((1, T), -jnp.inf, F32), jnp.zeros((1, T), F32),
                      jnp.zeros((MLA_V_DIM, T), F32)) for _ in range(heads))

        def k_step(kj, carry):
            return tuple(tile_update(h, qts[h], kj, carry[h], False) for h in range(heads))

        carry = lax.fori_loop(0, qi, k_step, init)
        outs = []
        for h in range(heads):
            m, l, acc = tile_update(h, qts[h], qi, carry[h], True)
            outs.append(acc / l)
        o = jnp.concatenate(outs, axis=0)
        q0 = pl.multiple_of(qi * T, T)
        o_ref[0, pl.ds(q0, T), :] = o.T.astype(o_ref.dtype)
        return 0

    lax.fori_loop(0, nq, q_tile, 0)


def _attention(qt, k, vt):
    b, nh, nt, _, T = qt.shape
    s = nt * T
    hp = 2
    return pl.pallas_call(
        _attn_kernel,
        grid=(b, nh // hp),
        in_specs=[
            pl.BlockSpec((1, hp, nt, MLA_QK_PAD, T), lambda i, j: (i, j, 0, 0, 0)),
            pl.BlockSpec((1, hp, s, MLA_QK_PAD), lambda i, j: (i, j, 0, 0)),
            pl.BlockSpec((1, hp, nt, MLA_V_DIM, T), lambda i, j: (i, j, 0, 0, 0)),
        ],
        out_specs=pl.BlockSpec((1, s, hp * MLA_V_DIM), lambda i, j: (i, 0, j)),
        out_shape=jax.ShapeDtypeStruct((b, s, nh * MLA_V_DIM), BF16),
        compiler_params=_params(("parallel", "parallel")),
        name="attention",
    )(qt, k, vt)


def _mix_kernel(x_ref, ys_ref, at_ref, gt_ref, gb_ref, ws_ref, wm_ref, wo_ref, o_ref):
    y_ssd = jnp.dot(ys_ref[...], ws_ref[...], preferred_element_type=F32)
    y_mla = jnp.dot(at_ref[...], wm_ref[...], preferred_element_type=F32)
    g = _sigmoid(gt_ref[...].astype(F32) + gb_ref[...])
    mix = g[:, :D_MODEL] * y_ssd + g[:, D_MODEL:] * y_mla
    o_ref[...] = x_ref[...] + jnp.dot(mix.astype(BF16), wo_ref[...], preferred_element_type=F32)


def _mix(x2d, ys, at, gt, gb, ws, wm, wo, tm):
    t, d = x2d.shape
    tile = lambda n: pl.BlockSpec((tm, n), lambda i: (i, 0))
    const = lambda a: pl.BlockSpec(a.shape, lambda i: (0, 0))
    return pl.pallas_call(
        _mix_kernel,
        grid=(t // tm,),
        in_specs=[tile(d), tile(ys.shape[1]), tile(at.shape[1]), tile(gt.shape[1]),
                  const(gb), const(ws), const(wm), const(wo)],
        out_specs=tile(d),
        out_shape=jax.ShapeDtypeStruct((t, d), F32),
        compiler_params=_params(("parallel",)),
        name="mix",
    )(x2d, ys, at, gt, gb, ws, wm, wo)


def _ffn_kernel(x_ref, g_ref, wup_ref, cw_ref, cb_ref, wdn_ref, o_ref, tail_ref):
    tm = x_ref.shape[1]

    @pl.when(pl.program_id(1) == 0)
    def _():
        tail_ref[...] = jnp.zeros_like(tail_ref)

    x = x_ref[0]
    h = _rms_rows(x, g_ref[...]).astype(BF16)

    def conv(c0):
        cs = slice(c0, c0 + FFN_COL_CHUNK)
        u = jnp.dot(h, wup_ref[:, cs], preferred_element_type=F32)
        ext = jnp.concatenate([tail_ref[:, cs], u], axis=0)
        tail_ref[:, cs] = u[tm - SUBLANES:]
        cw = cw_ref[:, cs]
        out = cb_ref[:, cs] + ext[SUBLANES:] * cw[FFN_CONV - 1:FFN_CONV]
        for k in range(1, FFN_CONV):
            out = out + ext[SUBLANES - k:SUBLANES - k + tm] * cw[FFN_CONV - 1 - k:FFN_CONV - k]
        return out

    acc = x
    for c in range(D_FF // FFN_COL_CHUNK):
        c0 = c * FFN_COL_CHUNK
        act = (_silu(conv(c0)) * conv(D_FF + c0)).astype(BF16)
        acc = acc + jnp.dot(act, wdn_ref[c0:c0 + FFN_COL_CHUNK, :], preferred_element_type=F32)
    o_ref[0] = acc


def _ffn(x, g, wup, cw, cb, wdn, tm):
    b, s, d = x.shape
    const = lambda a: pl.BlockSpec(a.shape, lambda i, j: (0, 0))
    return pl.pallas_call(
        _ffn_kernel,
        grid=(b, s // tm),
        in_specs=[pl.BlockSpec((1, tm, d), lambda i, j: (i, j, 0)),
                  const(g), const(wup), const(cw), const(cb), const(wdn)],
        out_specs=pl.BlockSpec((1, tm, d), lambda i, j: (i, j, 0)),
        out_shape=jax.ShapeDtypeStruct((b, s, d), F32),
        scratch_shapes=[pltpu.VMEM((SUBLANES, 2 * D_FF), F32)],
        compiler_params=_params(("parallel", "arbitrary")),
        name="ffn",
    )(x, g, wup, cw, cb, wdn)


def _rope_tables_t(seq):
    inv = 1.0 / (ROPE_THETA ** (jnp.arange(0, MLA_ROPE, 2, dtype=F32) / MLA_ROPE))
    ang = inv[:, None] * jnp.arange(seq, dtype=F32)[None, :]
    return jnp.cos(ang), jnp.sin(ang)


def _head_expand_matrix():
    r = jnp.arange(4 * SSD_HEADS)[:, None]
    c = jnp.arange(D_INNER)[None, :]
    return ((r < 3 * SSD_HEADS) & ((r % SSD_HEADS) == (c // SSD_HEAD_DIM))).astype(BF16)


def _layer(x, p):
    b, s, d = x.shape
    t = b * s
    x2d = x.reshape(t, d)
    w_in = p["w_in"]
    col = lambda a: a.reshape(-1, 1)
    rowv = lambda a: a.reshape(1, -1)
    g_mix = rowv(p["norm_mix_g"])

    tm_proj = min(1024, t)
    z = _norm_proj(x2d, g_mix, w_in[:, :OFF_Z].astype(BF16), tm_proj, 1024)
    xbc = _norm_proj(x2d, g_mix, w_in[:, OFF_Z:OFF_XBC].astype(BF16), tm_proj, 1024)
    gates = _norm_proj(x2d, g_mix, w_in[:, OFF_KR:].astype(BF16), tm_proj, 1024)

    ws_t = jnp.concatenate([w_in[:, OFF_DT:OFF_KVA], w_in[:, OFF_XBC:OFF_DT], w_in[:, OFF_KVA:OFF_KR]],
                           axis=1).T.astype(BF16)
    cos_t, sin_t = _rope_tables_t(s)
    qt, k, vt, dt_t = _mla_prep(
        x, g_mix, ws_t, col(p["q_a_norm_g"]), p["w_uq"].T.astype(BF16), col(p["kv_a_norm_g"]),
        p["w_ukv"].T.astype(BF16), col(p["q_norm_g"]), col(p["k_norm_g"]), cos_t, sin_t, min(512, s))

    y_ssd = _ssd(xbc.reshape(b, s, -1), z.reshape(b, s, -1), dt_t, p["conv_ssd_w"], rowv(p["conv_ssd_b"]),
                 col(p["dt_bias"]), col(p["a_log"]), rowv(jnp.repeat(p["d_skip"], SSD_HEAD_DIM)),
                 rowv(p["ssd_norm_g"]), _head_expand_matrix())
    attn = _attention(qt, k, vt)

    x1 = _mix(x2d, y_ssd.reshape(t, -1), attn.reshape(t, -1), gates, rowv(p["gate_b"]),
              p["w_ssd_proj"].astype(BF16), p["w_mla_proj"].astype(BF16), p["w_o"].astype(BF16), min(512, t))
    out = _ffn(x1.reshape(b, s, d), rowv(p["norm_ffn_g"]), p["w_up"].astype(BF16), p["conv_ffn_w"],
               rowv(p["conv_ffn_b"]), p["w_down"].astype(BF16), min(256, s))
    return out


def kernel(x, norm_mix_g, w_in, conv_ssd_w, conv_ssd_b, dt_bias, a_log, d_skip, ssd_norm_g, w_ssd_proj,
           q_a_norm_g, w_uq, kv_a_norm_g, w_ukv, q_norm_g, k_norm_g, w_mla_proj, gate_b, w_o, norm_ffn_g,
           w_up, conv_ffn_w, conv_ffn_b, w_down):
    params = dict(norm_mix_g=norm_mix_g, w_in=w_in, conv_ssd_w=conv_ssd_w, conv_ssd_b=conv_ssd_b,
                  dt_bias=dt_bias, a_log=a_log, d_skip=d_skip, ssd_norm_g=ssd_norm_g, w_ssd_proj=w_ssd_proj,
                  q_a_norm_g=q_a_norm_g, w_uq=w_uq, kv_a_norm_g=kv_a_norm_g, w_ukv=w_ukv, q_norm_g=q_norm_g,
                  k_norm_g=k_norm_g, w_mla_proj=w_mla_proj, gate_b=gate_b, w_o=w_o, norm_ffn_g=norm_ffn_g,
                  w_up=w_up, conv_ffn_w=conv_ffn_w, conv_ffn_b=conv_ffn_b, w_down=w_down)
    for i in range(w_in.shape[0]):
        x = _layer(x, {name: v[i] for name, v in params.items()})
    return x
```

```python
import functools

import jax
import jax.numpy as jnp
from jax import lax
from jax.experimental import pallas as pl
from jax.experimental.pallas import tpu as pltpu

F32 = jnp.float32
BF16 = jnp.bfloat16

D_MODEL = 1024
D_INNER = 2048
SSD_HEADS = 32
SSD_HEAD_DIM = 64
SSD_GROUPS = 4
SSD_STATE = 128
SSD_CONV = 4
SSD_CHUNK = 128
SSD_CONV_DIM = D_INNER + 2 * SSD_GROUPS * SSD_STATE
GROUP_WIDTH = D_INNER // SSD_GROUPS
MLA_HEADS = 16
MLA_Q_LORA = 256
MLA_KV_LORA = 128
MLA_NOPE = 64
MLA_ROPE = 32
MLA_QK_DIM = MLA_NOPE + MLA_ROPE
MLA_QK_PAD = 128
MLA_V_DIM = 64
ROPE_THETA = 10000.0
D_FF = 2816
FFN_CONV = 3
NORM_EPS = 1e-6
LOG2_E = 1.4426950408889634

OFF_Z = D_INNER
OFF_XBC = OFF_Z + SSD_CONV_DIM
OFF_DT = OFF_XBC + SSD_HEADS
OFF_QA = OFF_DT + MLA_Q_LORA
OFF_KVA = OFF_QA + MLA_KV_LORA
OFF_KR = OFF_KVA + MLA_ROPE

SUBLANES = 8
ATTN_TILE = 512
FFN_COL_CHUNK = 256
VMEM_LIMIT = 56 * 1024 * 1024


def _params(sem):
    return pltpu.CompilerParams(dimension_semantics=sem, vmem_limit_bytes=VMEM_LIMIT)


def _rms_rows(x, g):
    ms = jnp.mean(x * x, axis=-1, keepdims=True)
    return x * lax.rsqrt(ms + NORM_EPS) * g


def _rms_cols(x, g):
    ms = jnp.mean(x * x, axis=0, keepdims=True)
    return x * lax.rsqrt(ms + NORM_EPS) * g


def _sigmoid(x):
    return 1.0 / (1.0 + jnp.exp(-x))


def _silu(x):
    return x * _sigmoid(x)


def _norm_proj_kernel(x_ref, g_ref, w_ref, o_ref, h_ref):
    @pl.when(pl.program_id(1) == 0)
    def _():
        h_ref[...] = _rms_rows(x_ref[...], g_ref[...]).astype(BF16)

    o_ref[...] = jnp.dot(h_ref[...], w_ref[...], preferred_element_type=F32).astype(o_ref.dtype)


def _norm_proj(x2d, g, w, tm, tn):
    t, d = x2d.shape
    n = w.shape[1]
    return pl.pallas_call(
        _norm_proj_kernel,
        grid=(t // tm, n // tn),
        in_specs=[
            pl.BlockSpec((tm, d), lambda i, j: (i, 0)),
            pl.BlockSpec((1, d), lambda i, j: (0, 0)),
            pl.BlockSpec((d, tn), lambda i, j: (0, j)),
        ],
        out_specs=pl.BlockSpec((tm, tn), lambda i, j: (i, j)),
        out_shape=jax.ShapeDtypeStruct((t, n), BF16),
        scratch_shapes=[pltpu.VMEM((tm, d), BF16)],
        compiler_params=_params(("parallel", "arbitrary")),
        name="norm_proj",
    )(x2d, g, w)


def _rope_rows(x1, x2, cos, sin):
    return x1 * cos - x2 * sin, x1 * sin + x2 * cos


def _mla_prep_kernel(x_ref, g_ref, ws_ref, qag_ref, wuq_ref, kvg_ref, wukv_ref, qg_ref, kg_ref,
                     cos_ref, sin_ref, qt_ref, k_ref, vt_ref, dt_ref):
    tm = x_ref.shape[1]
    h = _rms_rows(x_ref[0], g_ref[...]).astype(BF16)
    st = lax.dot_general(ws_ref[...], h, (((1,), (1,)), ((), ())), preferred_element_type=F32)
    qa = st[0:MLA_Q_LORA]
    kva = st[MLA_Q_LORA:MLA_Q_LORA + MLA_KV_LORA]
    o = MLA_Q_LORA + MLA_KV_LORA
    dt_ref[0] = st[o:o + SSD_HEADS]
    kr = st[o + SSD_HEADS:o + SSD_HEADS + MLA_ROPE]

    qa_n = _rms_cols(qa, qag_ref[...]).astype(BF16)
    kva_n = _rms_cols(kva, kvg_ref[...]).astype(BF16)
    q_all = jnp.dot(wuq_ref[...], qa_n, preferred_element_type=F32)
    kv_all = jnp.dot(wukv_ref[...], kva_n, preferred_element_type=F32)

    cos = cos_ref[...]
    sin = sin_ref[...]
    qg = qg_ref[...]
    kg = kg_ref[...]
    half = MLA_ROPE // 2
    scale = MLA_QK_DIM ** -0.5 * LOG2_E
    kr_ss = jnp.sum(kr * kr, axis=0, keepdims=True)
    pad = jnp.zeros((MLA_QK_PAD - MLA_QK_DIM, tm), F32)
    n_sub = tm // ATTN_TILE
    for hd in range(MLA_HEADS):
        qh = _rms_cols(q_all[MLA_QK_DIM * hd:MLA_QK_DIM * (hd + 1)], qg) * scale
        r1, r2 = _rope_rows(qh[MLA_NOPE:MLA_NOPE + half], qh[MLA_NOPE + half:], cos, sin)
        qfull = jnp.concatenate([qh[:MLA_NOPE], r1, r2, pad], axis=0).astype(BF16)

        base = (MLA_NOPE + MLA_V_DIM) * hd
        kn = kv_all[base:base + MLA_NOPE]
        v = kv_all[base + MLA_NOPE:base + MLA_NOPE + MLA_V_DIM].astype(BF16)
        ss = (jnp.sum(kn * kn, axis=0, keepdims=True) + kr_ss) * (1.0 / MLA_QK_DIM)
        rs = lax.rsqrt(ss + NORM_EPS)
        kn = kn * rs * kg[:MLA_NOPE]
        krn = kr * rs * kg[MLA_NOPE:]
        r1, r2 = _rope_rows(krn[:half], krn[half:], cos, sin)
        kfull = jnp.concatenate([kn, r1, r2, pad], axis=0)
        k_ref[0, hd] = kfull.T.astype(BF16)
        for j in range(n_sub):
            sl = slice(j * ATTN_TILE, (j + 1) * ATTN_TILE)
            qt_ref[0, hd, j] = qfull[:, sl]
            vt_ref[0, hd, j] = v[:, sl]


def _mla_prep(x, g, ws_t, qag, wuq_t, kvg, wukv_t, qg, kg, cos_t, sin_t, tm):
    b, s, d = x.shape
    nt = s // ATTN_TILE
    n_sub = tm // ATTN_TILE
    const = lambda shape: pl.BlockSpec(shape, lambda i, j: (0,) * len(shape))
    return pl.pallas_call(
        _mla_prep_kernel,
        grid=(b, s // tm),
        in_specs=[
            pl.BlockSpec((1, tm, d), lambda i, j: (i, j, 0)),
            const(g.shape), const(ws_t.shape), const(qag.shape), const(wuq_t.shape),
            const(kvg.shape), const(wukv_t.shape), const(qg.shape), const(kg.shape),
            pl.BlockSpec((MLA_ROPE // 2, tm), lambda i, j: (0, j)),
            pl.BlockSpec((MLA_ROPE // 2, tm), lambda i, j: (0, j)),
        ],
        out_specs=[
            pl.BlockSpec((1, MLA_HEADS, n_sub, MLA_QK_PAD, ATTN_TILE), lambda i, j: (i, 0, j, 0, 0)),
            pl.BlockSpec((1, MLA_HEADS, tm, MLA_QK_PAD), lambda i, j: (i, 0, j, 0)),
            pl.BlockSpec((1, MLA_HEADS, n_sub, MLA_V_DIM, ATTN_TILE), lambda i, j: (i, 0, j, 0, 0)),
            pl.BlockSpec((1, SSD_HEADS, tm), lambda i, j: (i, 0, j)),
        ],
        out_shape=[
            jax.ShapeDtypeStruct((b, MLA_HEADS, nt, MLA_QK_PAD, ATTN_TILE), BF16),
            jax.ShapeDtypeStruct((b, MLA_HEADS, s, MLA_QK_PAD), BF16),
            jax.ShapeDtypeStruct((b, MLA_HEADS, nt, MLA_V_DIM, ATTN_TILE), BF16),
            jax.ShapeDtypeStruct((b, SSD_HEADS, s), F32),
        ],
        compiler_params=_params(("parallel", "parallel")),
        name="mla_prep",
    )(x, g, ws_t, qag, wuq_t, kvg, wukv_t, qg, kg, cos_t, sin_t)


def _split3_rows(v):
    hi = v.astype(BF16).astype(F32)
    r1 = v - hi
    mid = r1.astype(BF16).astype(F32)
    lo = (r1 - mid).astype(BF16).astype(F32)
    return jnp.concatenate([hi, mid, lo, jnp.zeros_like(v)], axis=0)


def _ssd_kernel(xbc_ref, z_ref, dt_ref, cw_ref, cb_ref, dtb_ref, alog_ref, dskip_ref, ng_ref, e3_ref,
                o_ref, xprev_ref, state_ref):
    L = SSD_CHUNK

    @pl.when(pl.program_id(1) == 0)
    def _():
        xprev_ref[...] = jnp.zeros_like(xprev_ref)
        state_ref[...] = jnp.zeros_like(state_ref)

    cur = xbc_ref[0].astype(F32)
    ext = jnp.concatenate([xprev_ref[...], cur], axis=0)
    xprev_ref[...] = cur[L - SUBLANES:]
    cw = cw_ref[...]
    acc = cb_ref[...] + ext[SUBLANES:] * cw[SSD_CONV - 1:SSD_CONV]
    for k in range(1, SSD_CONV):
        acc = acc + ext[SUBLANES - k:SUBLANES - k + L] * cw[SSD_CONV - 1 - k:SSD_CONV - k]
    act = _silu(acc)
    xs = act[:, :D_INNER]
    bm = act[:, D_INNER:D_INNER + SSD_GROUPS * SSD_STATE]
    cm = act[:, D_INNER + SSD_GROUPS * SSD_STATE:]

    dt_raw = dt_ref[0] + dtb_ref[...]
    dt_t = jnp.maximum(dt_raw, 0.0) + jnp.log(1.0 + jnp.exp(-jnp.abs(dt_raw)))
    a_t = dt_t * (-jnp.exp(alog_ref[...]))
    row = lax.broadcasted_iota(jnp.int32, (L, L), 0)
    col = lax.broadcasted_iota(jnp.int32, (L, L), 1)
    triu = (row <= col).astype(F32)
    acum_t = jnp.dot(a_t, triu, preferred_element_type=F32, precision=lax.Precision.HIGHEST)
    w2_t = dt_t * jnp.exp(acum_t[:, L - 1:L] - acum_t)
    ea_t = jnp.exp(acum_t)

    def expand(v_t):
        p = _split3_rows(v_t).T.astype(BF16)
        return jnp.dot(p, e3_ref[...], preferred_element_type=F32)

    dt_x = expand(dt_t)
    w2_x = expand(w2_t)
    ea_x = expand(ea_t)
    acum_c = jnp.concatenate([acum_t, jnp.zeros((L - SSD_HEADS, L), F32)], axis=0).T

    xdt = xs * dt_x
    xw2 = (xs * w2_x).astype(BF16)
    causal = row >= col
    left = col < SSD_HEAD_DIM
    heads_per_group = SSD_HEADS // SSD_GROUPS
    y_parts = []
    for g in range(SSD_GROUPS):
        gs = slice(g * GROUP_WIDTH, (g + 1) * GROUP_WIDTH)
        b_g = bm[:, g * SSD_STATE:(g + 1) * SSD_STATE]
        c_g = cm[:, g * SSD_STATE:(g + 1) * SSD_STATE].astype(BF16)
        scores = lax.dot_general(c_g, b_g.astype(BF16), (((1,), (1,)), ((), ())),
                                 preferred_element_type=F32)
        prev = state_ref[g]
        y_g = jnp.dot(c_g, prev.astype(BF16), preferred_element_type=F32) * ea_x[:, gs]
        diag = []
        for pair in range(heads_per_group // 2):
            ms = []
            for e in range(2):
                hd = g * heads_per_group + 2 * pair + e
                seg = acum_c[:, hd:hd + 1] - acum_t[hd:hd + 1, :]
                ms.append(scores * jnp.exp(jnp.where(causal, seg, -jnp.inf)))
            m2 = jnp.concatenate(ms, axis=1).astype(BF16)
            c0 = g * GROUP_WIDTH + pair * 2 * SSD_HEAD_DIM
            x2 = xdt[:, c0:c0 + 2 * SSD_HEAD_DIM]
            r2 = jnp.concatenate([jnp.where(left, x2, 0.0), jnp.where(left, 0.0, x2)], axis=0).astype(BF16)
            diag.append(jnp.dot(m2, r2, preferred_element_type=F32))
        y_parts.append(y_g + jnp.concatenate(diag, axis=1))
        new_state = jnp.dot(b_g.T.astype(BF16), xw2[:, gs], preferred_element_type=F32)
        state_ref[g] = prev * ea_x[L - 1:L, gs] + new_state

    y = jnp.concatenate(y_parts, axis=1) + xs * dskip_ref[...]
    y = y * _silu(z_ref[0].astype(F32))
    ng = ng_ref[...]
    outs = []
    for g in range(SSD_GROUPS):
        gs = slice(g * GROUP_WIDTH, (g + 1) * GROUP_WIDTH)
        outs.append(_rms_rows(y[:, gs], ng[:, gs]))
    o_ref[0] = jnp.concatenate(outs, axis=1).astype(o_ref.dtype)


def _ssd(xbc, z, dt_t, cw, cb, dtb, alog, dskip_x, ng, e3):
    b, s, _ = xbc.shape
    L = SSD_CHUNK
    const = lambda shape: pl.BlockSpec(shape, lambda i, j: (0,) * len(shape))
    return pl.pallas_call(
        _ssd_kernel,
        grid=(b, s // L),
        in_specs=[
            pl.BlockSpec((1, L, SSD_CONV_DIM), lambda i, j: (i, j, 0)),
            pl.BlockSpec((1, L, D_INNER), lambda i, j: (i, j, 0)),
            pl.BlockSpec((1, SSD_HEADS, L), lambda i, j: (i, 0, j)),
            const(cw.shape), const(cb.shape), const(dtb.shape), const(alog.shape),
            const(dskip_x.shape), const(ng.shape), const(e3.shape),
        ],
        out_specs=pl.BlockSpec((1, L, D_INNER), lambda i, j: (i, j, 0)),
        out_shape=jax.ShapeDtypeStruct((b, s, D_INNER), BF16),
        scratch_shapes=[
            pltpu.VMEM((SUBLANES, SSD_CONV_DIM), F32),
            pltpu.VMEM((SSD_GROUPS, SSD_STATE, GROUP_WIDTH), F32),
        ],
        compiler_params=_params(("parallel", "arbitrary")),
        name="ssd",
    )(xbc, z, dt_t, cw, cb, dtb, alog, dskip_x, ng, e3)


def _attn_kernel(qt_ref, k_ref, vt_ref, o_ref, acc_ref, s_ref):
    nq = qt_ref.shape[2]
    heads = qt_ref.shape[1]
    T = ATTN_TILE
    row = lax.broadcasted_iota(jnp.int32, (T, T), 0)
    col = lax.broadcasted_iota(jnp.int32, (T, T), 1)
    diag_ok = row <= col

    def store_scores(qi, kj, slot):
        for h in range(heads):
            kt = k_ref[0, h, kj * T:(kj + 1) * T, :]
            s_ref[slot, h] = jnp.dot(kt, qt_ref[0, h, qi], preferred_element_type=F32)

    def softmax_update(h, kj, slot, carry, masked):
        m, l = carry
        s = s_ref[slot, h]
        if masked:
            s = jnp.where(diag_ok, s, -jnp.inf)
        m_new = jnp.maximum(m, jnp.max(s, axis=0, keepdims=True))
        p = jnp.exp2(s - m_new)
        alpha = jnp.exp2(m - m_new)
        l = alpha * l + jnp.sum(p, axis=0, keepdims=True)
        pv = jnp.dot(vt_ref[0, h, kj], p.astype(BF16), preferred_element_type=F32)
        acc_ref[h] = alpha * acc_ref[h] + pv
        return m_new, l

    pairs = [(qi, kj) for qi in range(nq) for kj in range(qi + 1)]
    store_scores(0, 0, 0)
    carry = None
    for idx, (qi, kj) in enumerate(pairs):
        slot = idx % 2
        if kj == 0:
            acc_ref[...] = jnp.zeros_like(acc_ref)
            carry = tuple((jnp.full((1, T), -jnp.inf, F32), jnp.zeros((1, T), F32)) for _ in range(heads))
        if idx + 1 < len(pairs):
            store_scores(*pairs[idx + 1], 1 - slot)
        carry = tuple(softmax_update(h, kj, slot, carry[h], kj == qi) for h in range(heads))
        if kj == qi:
            outs = [acc_ref[h] * (1.0 / carry[h][1]) for h in range(heads)]
            o = jnp.concatenate(outs, axis=0)
            o_ref[0, qi * T:(qi + 1) * T, :] = o.T.astype(o_ref.dtype)


def _attention(qt, k, vt):
    b, nh, nt, _, T = qt.shape
    s = nt * T
    hp = 2
    return pl.pallas_call(
        _attn_kernel,
        grid=(b, nh // hp),
        in_specs=[
            pl.BlockSpec((1, hp, nt, MLA_QK_PAD, T), lambda i, j: (i, j, 0, 0, 0)),
            pl.BlockSpec((1, hp, s, MLA_QK_PAD), lambda i, j: (i, j, 0, 0)),
            pl.BlockSpec((1, hp, nt, MLA_V_DIM, T), lambda i, j: (i, j, 0, 0, 0)),
        ],
        out_specs=pl.BlockSpec((1, s, hp * MLA_V_DIM), lambda i, j: (i, 0, j)),
        out_shape=jax.ShapeDtypeStruct((b, s, nh * MLA_V_DIM), BF16),
        scratch_shapes=[pltpu.VMEM((hp, MLA_V_DIM, T), F32), pltpu.VMEM((2, hp, T, T), F32)],
        compiler_params=_params(("parallel", "parallel")),
        name="attention",
    )(qt, k, vt)


def _mix_kernel(x_ref, ys_ref, at_ref, gt_ref, gb_ref, ws_ref, wm_ref, wo_ref, o_ref):
    y_ssd = jnp.dot(ys_ref[...], ws_ref[...], preferred_element_type=F32)
    y_mla = jnp.dot(at_ref[...], wm_ref[...], preferred_element_type=F32)
    g = _sigmoid(gt_ref[...].astype(F32) + gb_ref[...])
    mix = g[:, :D_MODEL] * y_ssd + g[:, D_MODEL:] * y_mla
    o_ref[...] = x_ref[...] + jnp.dot(mix.astype(BF16), wo_ref[...], preferred_element_type=F32)


def _mix(x2d, ys, at, gt, gb, ws, wm, wo, tm):
    t, d = x2d.shape
    tile = lambda n: pl.BlockSpec((tm, n), lambda i: (i, 0))
    const = lambda a: pl.BlockSpec(a.shape, lambda i: (0, 0))
    return pl.pallas_call(
        _mix_kernel,
        grid=(t // tm,),
        in_specs=[tile(d), tile(ys.shape[1]), tile(at.shape[1]), tile(gt.shape[1]),
                  const(gb), const(ws), const(wm), const(wo)],
        out_specs=tile(d),
        out_shape=jax.ShapeDtypeStruct((t, d), F32),
        compiler_params=_params(("parallel",)),
        name="mix",
    )(x2d, ys, at, gt, gb, ws, wm, wo)


def _ffn_kernel(x_ref, g_ref, wup_ref, cw_ref, cb_ref, wdn_ref, o_ref, tail_ref):
    tm = x_ref.shape[1]

    @pl.when(pl.program_id(1) == 0)
    def _():
        tail_ref[...] = jnp.zeros_like(tail_ref)

    x = x_ref[0]
    h = _rms_rows(x, g_ref[...]).astype(BF16)

    def conv(c0):
        cs = slice(c0, c0 + FFN_COL_CHUNK)
        u = jnp.dot(h, wup_ref[:, cs], preferred_element_type=F32)
        ext = jnp.concatenate([tail_ref[:, cs], u], axis=0)
        tail_ref[:, cs] = u[tm - SUBLANES:]
        cw = cw_ref[:, cs]
        out = cb_ref[:, cs] + ext[SUBLANES:] * cw[FFN_CONV - 1:FFN_CONV]
        for k in range(1, FFN_CONV):
            out = out + ext[SUBLANES - k:SUBLANES - k + tm] * cw[FFN_CONV - 1 - k:FFN_CONV - k]
        return out

    acc = x
    for c in range(D_FF // FFN_COL_CHUNK):
        c0 = c * FFN_COL_CHUNK
        act = (_silu(conv(c0)) * conv(D_FF + c0)).astype(BF16)
        acc = acc + jnp.dot(act, wdn_ref[c0:c0 + FFN_COL_CHUNK, :], preferred_element_type=F32)
    o_ref[0] = acc


def _ffn(x, g, wup, cw, cb, wdn, tm):
    b, s, d = x.shape
    const = lambda a: pl.BlockSpec(a.shape, lambda i, j: (0, 0))
    return pl.pallas_call(
        _ffn_kernel,
        grid=(b, s // tm),
        in_specs=[pl.BlockSpec((1, tm, d), lambda i, j: (i, j, 0)),
                  const(g), const(wup), const(cw), const(cb), const(wdn)],
        out_specs=pl.BlockSpec((1, tm, d), lambda i, j: (i, j, 0)),
        out_shape=jax.ShapeDtypeStruct((b, s, d), F32),
        scratch_shapes=[pltpu.VMEM((SUBLANES, 2 * D_FF), F32)],
        compiler_params=_params(("parallel", "arbitrary")),
        name="ffn",
    )(x, g, wup, cw, cb, wdn)


def _rope_tables_t(seq):
    inv = 1.0 / (ROPE_THETA ** (jnp.arange(0, MLA_ROPE, 2, dtype=F32) / MLA_ROPE))
    ang = inv[:, None] * jnp.arange(seq, dtype=F32)[None, :]
    return jnp.cos(ang), jnp.sin(ang)


def _head_expand_matrix():
    r = jnp.arange(4 * SSD_HEADS)[:, None]
    c = jnp.arange(D_INNER)[None, :]
    return ((r < 3 * SSD_HEADS) & ((r % SSD_HEADS) == (c // SSD_HEAD_DIM))).astype(BF16)


def _layer(x, p):
    b, s, d = x.shape
    t = b * s
    x2d = x.reshape(t, d)
    w_in = p["w_in"]
    col = lambda a: a.reshape(-1, 1)
    rowv = lambda a: a.reshape(1, -1)
    g_mix = rowv(p["norm_mix_g"])

    tm_proj = min(1024, t)
    z = _norm_proj(x2d, g_mix, w_in[:, :OFF_Z].astype(BF16), tm_proj, 1024)
    xbc = _norm_proj(x2d, g_mix, w_in[:, OFF_Z:OFF_XBC].astype(BF16), tm_proj, 1024)
    gates = _norm_proj(x2d, g_mix, w_in[:, OFF_KR:].astype(BF16), tm_proj, 1024)

    ws_t = jnp.concatenate([w_in[:, OFF_DT:OFF_KVA], w_in[:, OFF_XBC:OFF_DT], w_in[:, OFF_KVA:OFF_KR]],
                           axis=1).T.astype(BF16)
    cos_t, sin_t = _rope_tables_t(s)
    qt, k, vt, dt_t = _mla_prep(
        x, g_mix, ws_t, col(p["q_a_norm_g"]), p["w_uq"].T.astype(BF16), col(p["kv_a_norm_g"]),
        p["w_ukv"].T.astype(BF16), col(p["q_norm_g"]), col(p["k_norm_g"]), cos_t, sin_t, min(512, s))

    y_ssd = _ssd(xbc.reshape(b, s, -1), z.reshape(b, s, -1), dt_t, p["conv_ssd_w"], rowv(p["conv_ssd_b"]),
                 col(p["dt_bias"]), col(p["a_log"]), rowv(jnp.repeat(p["d_skip"], SSD_HEAD_DIM)),
                 rowv(p["ssd_norm_g"]), _head_expand_matrix())
    attn = _attention(qt, k, vt)

    x1 = _mix(x2d, y_ssd.reshape(t, -1), attn.reshape(t, -1), gates, rowv(p["gate_b"]),
              p["w_ssd_proj"].astype(BF16), p["w_mla_proj"].astype(BF16), p["w_o"].astype(BF16), min(512, t))
    out = _ffn(x1.reshape(b, s, d), rowv(p["norm_ffn_g"]), p["w_up"].astype(BF16), p["conv_ffn_w"],
               rowv(p["conv_ffn_b"]), p["w_down"].astype(BF16), min(256, s))
    return out


def kernel(x, norm_mix_g, w_in, conv_ssd_w, conv_ssd_b, dt_bias, a_log, d_skip, ssd_norm_g, w_ssd_proj,
           q_a_norm_g, w_uq, kv_a_norm_g, w_ukv, q_norm_g, k_norm_g, w_mla_proj, gate_b, w_o, norm_ffn_g,
           w_up, conv_ffn_w, conv_ffn_b, w_down):
    params = dict(norm_mix_g=norm_mix_g, w_in=w_in, conv_ssd_w=conv_ssd_w, conv_ssd_b=conv_ssd_b,
                  dt_bias=dt_bias, a_log=a_log, d_skip=d_skip, ssd_norm_g=ssd_norm_g, w_ssd_proj=w_ssd_proj,
                  q_a_norm_g=q_a_norm_g, w_uq=w_uq, kv_a_norm_g=kv_a_norm_g, w_ukv=w_ukv, q_norm_g=q_norm_g,
                  k_norm_g=k_norm_g, w_mla_proj=w_mla_proj, gate_b=gate_b, w_o=w_o, norm_ffn_g=norm_ffn_g,
                  w_up=w_up, conv_ffn_w=conv_ffn_w, conv_ffn_b=conv_ffn_b, w_down=w_down)
    for i in range(w_in.shape[0]):
        x = _layer(x, {name: v[i] for name, v in params.items()})
    return x
```

```python
import functools

import jax
import jax.numpy as jnp
from jax import lax
from jax.experimental import pallas as pl
from jax.experimental.pallas import tpu as pltpu

F32 = jnp.float32
BF16 = jnp.bfloat16

D_MODEL = 1024
D_INNER = 2048
SSD_HEADS = 32
SSD_HEAD_DIM = 64
SSD_GROUPS = 4
SSD_STATE = 128
SSD_CONV = 4
SSD_CHUNK = 128
SSD_CONV_DIM = D_INNER + 2 * SSD_GROUPS * SSD_STATE
GROUP_WIDTH = D_INNER // SSD_GROUPS
MLA_HEADS = 16
MLA_Q_LORA = 256
MLA_KV_LORA = 128
MLA_NOPE = 64
MLA_ROPE = 32
MLA_QK_DIM = MLA_NOPE + MLA_ROPE
MLA_QK_PAD = 128
MLA_V_DIM = 64
ROPE_THETA = 10000.0
D_FF = 2816
FFN_CONV = 3
NORM_EPS = 1e-6
LOG2_E = 1.4426950408889634

OFF_Z = D_INNER
OFF_XBC = OFF_Z + SSD_CONV_DIM
OFF_DT = OFF_XBC + SSD_HEADS
OFF_QA = OFF_DT + MLA_Q_LORA
OFF_KVA = OFF_QA + MLA_KV_LORA
OFF_KR = OFF_KVA + MLA_ROPE

SUBLANES = 8
BF16_ROWS = 16
ATTN_TILE = 512
FFN_COL_CHUNK = 256
PROJ_COL_CHUNK = 1024
VMEM_LIMIT = 56 * 1024 * 1024


def _params(sem):
    return pltpu.CompilerParams(dimension_semantics=sem, vmem_limit_bytes=VMEM_LIMIT)


def _rms_rows(x, g):
    ms = jnp.mean(x * x, axis=-1, keepdims=True)
    return x * lax.rsqrt(ms + NORM_EPS) * g


def _rms_cols(x, g):
    ms = jnp.mean(x * x, axis=0, keepdims=True)
    return x * lax.rsqrt(ms + NORM_EPS) * g


def _sigmoid(x):
    return 1.0 / (1.0 + jnp.exp(-x))


def _silu(x):
    return x * _sigmoid(x)


def _norm_proj_kernel(x_ref, g_ref, w_ref, *o_refs):
    h = _rms_rows(x_ref[...], g_ref[...]).astype(BF16)
    c0 = 0
    for o_ref in o_refs:
        for j in range(o_ref.shape[1] // PROJ_COL_CHUNK):
            cs = slice(j * PROJ_COL_CHUNK, (j + 1) * PROJ_COL_CHUNK)
            ws = slice(c0 + cs.start, c0 + cs.stop)
            o_ref[:, cs] = jnp.dot(h, w_ref[:, ws], preferred_element_type=F32).astype(o_ref.dtype)
        c0 += o_ref.shape[1]


def _norm_proj(x2d, g, w, widths, tm):
    t, d = x2d.shape
    return pl.pallas_call(
        _norm_proj_kernel,
        grid=(t // tm,),
        in_specs=[
            pl.BlockSpec((tm, d), lambda i: (i, 0)),
            pl.BlockSpec((1, d), lambda i: (0, 0)),
            pl.BlockSpec(w.shape, lambda i: (0, 0), pipeline_mode=pl.Buffered(1)),
        ],
        out_specs=[pl.BlockSpec((tm, n), lambda i: (i, 0)) for n in widths],
        out_shape=[jax.ShapeDtypeStruct((t, n), BF16) for n in widths],
        compiler_params=_params(("parallel",)),
        name="norm_proj",
    )(x2d, g, w)


def _rope_rows(x1, x2, cos, sin):
    return x1 * cos - x2 * sin, x1 * sin + x2 * cos


def _mla_prep_kernel(x_ref, g_ref, ws_ref, qag_ref, wuq_ref, kvg_ref, wukv_ref, qg_ref, kg_ref,
                     cos_ref, sin_ref, qt_ref, k_ref, vt_ref, dt_ref):
    tm = x_ref.shape[1]
    h = _rms_rows(x_ref[0], g_ref[...]).astype(BF16)
    st = lax.dot_general(ws_ref[...], h, (((1,), (1,)), ((), ())), preferred_element_type=F32)
    qa = st[0:MLA_Q_LORA]
    kva = st[MLA_Q_LORA:MLA_Q_LORA + MLA_KV_LORA]
    o = MLA_Q_LORA + MLA_KV_LORA
    dt_ref[0] = st[o:o + SSD_HEADS]
    kr = st[o + SSD_HEADS:o + SSD_HEADS + MLA_ROPE]

    qa_n = _rms_cols(qa, qag_ref[...]).astype(BF16)
    kva_n = _rms_cols(kva, kvg_ref[...]).astype(BF16)
    q_all = jnp.dot(wuq_ref[...], qa_n, preferred_element_type=F32)
    kv_all = jnp.dot(wukv_ref[...], kva_n, preferred_element_type=F32)

    cos = cos_ref[...]
    sin = sin_ref[...]
    qg = qg_ref[...]
    kg = kg_ref[...]
    half = MLA_ROPE // 2
    scale = MLA_QK_DIM ** -0.5 * LOG2_E
    kr_ss = jnp.sum(kr * kr, axis=0, keepdims=True)
    pad = jnp.zeros((MLA_QK_PAD - MLA_QK_DIM, tm), F32)
    n_sub = tm // ATTN_TILE
    for hd in range(MLA_HEADS):
        qh = _rms_cols(q_all[MLA_QK_DIM * hd:MLA_QK_DIM * (hd + 1)], qg) * scale
        r1, r2 = _rope_rows(qh[MLA_NOPE:MLA_NOPE + half], qh[MLA_NOPE + half:], cos, sin)
        qfull = jnp.concatenate([qh[:MLA_NOPE], r1, r2, pad], axis=0).astype(BF16)

        base = (MLA_NOPE + MLA_V_DIM) * hd
        kn = kv_all[base:base + MLA_NOPE]
        v = kv_all[base + MLA_NOPE:base + MLA_NOPE + MLA_V_DIM].astype(BF16)
        ss = (jnp.sum(kn * kn, axis=0, keepdims=True) + kr_ss) * (1.0 / MLA_QK_DIM)
        rs = lax.rsqrt(ss + NORM_EPS)
        kn = kn * rs * kg[:MLA_NOPE]
        krn = kr * rs * kg[MLA_NOPE:]
        r1, r2 = _rope_rows(krn[:half], krn[half:], cos, sin)
        kfull = jnp.concatenate([kn, r1, r2, pad], axis=0)
        k_ref[0, hd] = kfull.T.astype(BF16)
        for j in range(n_sub):
            sl = slice(j * ATTN_TILE, (j + 1) * ATTN_TILE)
            qt_ref[0, hd, j] = qfull[:, sl]
            vt_ref[0, hd, j] = v[:, sl]


def _mla_prep(x, g, ws_t, qag, wuq_t, kvg, wukv_t, qg, kg, cos_t, sin_t, tm):
    b, s, d = x.shape
    nt = s // ATTN_TILE
    n_sub = tm // ATTN_TILE
    const = lambda shape: pl.BlockSpec(shape, lambda i, j: (0,) * len(shape))
    return pl.pallas_call(
        _mla_prep_kernel,
        grid=(b, s // tm),
        in_specs=[
            pl.BlockSpec((1, tm, d), lambda i, j: (i, j, 0)),
            const(g.shape), const(ws_t.shape), const(qag.shape), const(wuq_t.shape),
            const(kvg.shape), const(wukv_t.shape), const(qg.shape), const(kg.shape),
            pl.BlockSpec((MLA_ROPE // 2, tm), lambda i, j: (0, j)),
            pl.BlockSpec((MLA_ROPE // 2, tm), lambda i, j: (0, j)),
        ],
        out_specs=[
            pl.BlockSpec((1, MLA_HEADS, n_sub, MLA_QK_PAD, ATTN_TILE), lambda i, j: (i, 0, j, 0, 0)),
            pl.BlockSpec((1, MLA_HEADS, tm, MLA_QK_PAD), lambda i, j: (i, 0, j, 0)),
            pl.BlockSpec((1, MLA_HEADS, n_sub, MLA_V_DIM, ATTN_TILE), lambda i, j: (i, 0, j, 0, 0)),
            pl.BlockSpec((1, SSD_HEADS, tm), lambda i, j: (i, 0, j)),
        ],
        out_shape=[
            jax.ShapeDtypeStruct((b, MLA_HEADS, nt, MLA_QK_PAD, ATTN_TILE), BF16),
            jax.ShapeDtypeStruct((b, MLA_HEADS, s, MLA_QK_PAD), BF16),
            jax.ShapeDtypeStruct((b, MLA_HEADS, nt, MLA_V_DIM, ATTN_TILE), BF16),
            jax.ShapeDtypeStruct((b, SSD_HEADS, s), F32),
        ],
        compiler_params=_params(("parallel", "parallel")),
        name="mla_prep",
    )(x, g, ws_t, qag, wuq_t, kvg, wukv_t, qg, kg, cos_t, sin_t)


def _split3_rows(v):
    hi = v.astype(BF16).astype(F32)
    r1 = v - hi
    mid = r1.astype(BF16).astype(F32)
    lo = (r1 - mid).astype(BF16).astype(F32)
    return jnp.concatenate([hi, mid, lo, jnp.zeros_like(v)], axis=0)


def _ssd_kernel(xbc_ref, z_ref, dt_ref, cw_ref, cb_ref, dtb_ref, alog_ref, dskip_ref, ng_ref, e3_ref, shift_ref,
                o_ref, xprev_ref, state_ref):
    L = SSD_CHUNK

    @pl.when(pl.program_id(1) == 0)
    def _():
        xprev_ref[...] = jnp.zeros_like(xprev_ref)
        state_ref[...] = jnp.zeros_like(state_ref)

    cur = xbc_ref[0]
    ext = jnp.concatenate([xprev_ref[...], cur], axis=0)
    xprev_ref[...] = cur[L - BF16_ROWS:]
    shifted = jnp.dot(shift_ref[...], ext, preferred_element_type=F32)
    cw = cw_ref[...]
    acc = cb_ref[...] + cur.astype(F32) * cw[SSD_CONV - 1:SSD_CONV]
    for k in range(1, SSD_CONV):
        acc = acc + shifted[(k - 1) * L:k * L] * cw[SSD_CONV - 1 - k:SSD_CONV - k]
    act = _silu(acc)
    xs = act[:, :D_INNER]
    bm = act[:, D_INNER:D_INNER + SSD_GROUPS * SSD_STATE]
    cm = act[:, D_INNER + SSD_GROUPS * SSD_STATE:]

    dt_raw = dt_ref[0] + dtb_ref[...]
    dt_t = jnp.maximum(dt_raw, 0.0) + jnp.log(1.0 + jnp.exp(-jnp.abs(dt_raw)))
    a_t = dt_t * (-jnp.exp(alog_ref[...]))
    row = lax.broadcasted_iota(jnp.int32, (L, L), 0)
    col = lax.broadcasted_iota(jnp.int32, (L, L), 1)
    triu = (row <= col).astype(F32)
    acum_t = jnp.dot(a_t, triu, preferred_element_type=F32, precision=lax.Precision.HIGHEST)
    w2_t = dt_t * jnp.exp(acum_t[:, L - 1:L] - acum_t)
    ea_t = jnp.exp(acum_t)

    def expand(v_t):
        p = _split3_rows(v_t).T.astype(BF16)
        return jnp.dot(p, e3_ref[...], preferred_element_type=F32)

    dt_x = expand(dt_t)
    w2_x = expand(w2_t)
    ea_x = expand(ea_t)
    acum_c = jnp.concatenate([acum_t, jnp.zeros((L - SSD_HEADS, L), F32)], axis=0).T

    xdt = xs * dt_x
    xw2 = (xs * w2_x).astype(BF16)
    causal = row >= col
    left = col < SSD_HEAD_DIM
    heads_per_group = SSD_HEADS // SSD_GROUPS
    y_parts = []
    for g in range(SSD_GROUPS):
        gs = slice(g * GROUP_WIDTH, (g + 1) * GROUP_WIDTH)
        b_g = bm[:, g * SSD_STATE:(g + 1) * SSD_STATE]
        c_g = cm[:, g * SSD_STATE:(g + 1) * SSD_STATE].astype(BF16)
        scores = lax.dot_general(c_g, b_g.astype(BF16), (((1,), (1,)), ((), ())),
                                 preferred_element_type=F32)
        prev = state_ref[g]
        y_g = jnp.dot(c_g, prev.astype(BF16), preferred_element_type=F32) * ea_x[:, gs]
        diag = []
        for pair in range(heads_per_group // 2):
            ms = []
            for e in range(2):
                hd = g * heads_per_group + 2 * pair + e
                seg = acum_c[:, hd:hd + 1] - acum_t[hd:hd + 1, :]
                ms.append(scores * jnp.exp(jnp.where(causal, seg, -jnp.inf)))
            m2 = jnp.concatenate(ms, axis=1).astype(BF16)
            c0 = g * GROUP_WIDTH + pair * 2 * SSD_HEAD_DIM
            x2 = xdt[:, c0:c0 + 2 * SSD_HEAD_DIM]
            r2 = jnp.concatenate([jnp.where(left, x2, 0.0), jnp.where(left, 0.0, x2)], axis=0).astype(BF16)
            diag.append(jnp.dot(m2, r2, preferred_element_type=F32))
        y_parts.append(y_g + jnp.concatenate(diag, axis=1))
        new_state = jnp.dot(b_g.T.astype(BF16), xw2[:, gs], preferred_element_type=F32)
        state_ref[g] = prev * ea_x[L - 1:L, gs] + new_state

    y = jnp.concatenate(y_parts, axis=1) + xs * dskip_ref[...]
    y = y * _silu(z_ref[0].astype(F32))
    ng = ng_ref[...]
    outs = []
    for g in range(SSD_GROUPS):
        gs = slice(g * GROUP_WIDTH, (g + 1) * GROUP_WIDTH)
        outs.append(_rms_rows(y[:, gs], ng[:, gs]))
    o_ref[0] = jnp.concatenate(outs, axis=1).astype(o_ref.dtype)


def _conv_shift_matrix():
    r = jnp.arange((SSD_CONV - 1) * SSD_CHUNK)[:, None]
    c = jnp.arange(BF16_ROWS + SSD_CHUNK)[None, :]
    return (c == BF16_ROWS + (r % SSD_CHUNK) - (r // SSD_CHUNK + 1)).astype(BF16)


def _ssd(xbc, z, dt_t, cw, cb, dtb, alog, dskip_x, ng, e3):
    b, s, _ = xbc.shape
    shift = _conv_shift_matrix()
    L = SSD_CHUNK
    const = lambda shape: pl.BlockSpec(shape, lambda i, j: (0,) * len(shape))
    return pl.pallas_call(
        _ssd_kernel,
        grid=(b, s // L),
        in_specs=[
            pl.BlockSpec((1, L, SSD_CONV_DIM), lambda i, j: (i, j, 0)),
            pl.BlockSpec((1, L, D_INNER), lambda i, j: (i, j, 0)),
            pl.BlockSpec((1, SSD_HEADS, L), lambda i, j: (i, 0, j)),
            const(cw.shape), const(cb.shape), const(dtb.shape), const(alog.shape),
            const(dskip_x.shape), const(ng.shape), const(e3.shape), const(shift.shape),
        ],
        out_specs=pl.BlockSpec((1, L, D_INNER), lambda i, j: (i, j, 0)),
        out_shape=jax.ShapeDtypeStruct((b, s, D_INNER), BF16),
        scratch_shapes=[
            pltpu.VMEM((BF16_ROWS, SSD_CONV_DIM), BF16),
            pltpu.VMEM((SSD_GROUPS, SSD_STATE, GROUP_WIDTH), F32),
        ],
        compiler_params=_params(("parallel", "arbitrary")),
        name="ssd",
    )(xbc, z, dt_t, cw, cb, dtb, alog, dskip_x, ng, e3, shift)


def _attn_kernel(qt_ref, k_ref, vt_ref, o_ref, acc_ref, s_ref):
    nq = qt_ref.shape[2]
    heads = qt_ref.shape[1]
    T = ATTN_TILE
    row = lax.broadcasted_iota(jnp.int32, (T, T), 0)
    col = lax.broadcasted_iota(jnp.int32, (T, T), 1)
    diag_ok = row <= col

    def store_scores(qi, kj, slot):
        for h in range(heads):
            kt = k_ref[0, h, kj * T:(kj + 1) * T, :]
            s_ref[slot, h] = jnp.dot(kt, qt_ref[0, h, qi], preferred_element_type=F32)

    def softmax_update(h, kj, slot, carry, masked):
        m, l = carry
        s = s_ref[slot, h]
        if masked:
            s = jnp.where(diag_ok, s, -jnp.inf)
        m_new = jnp.maximum(m, jnp.max(s, axis=0, keepdims=True))
        p = jnp.exp2(s - m_new)
        alpha = jnp.exp2(m - m_new)
        l = alpha * l + jnp.sum(p, axis=0, keepdims=True)
        pv = jnp.dot(vt_ref[0, h, kj], p.astype(BF16), preferred_element_type=F32)
        acc_ref[h] = alpha * acc_ref[h] + pv
        return m_new, l

    pairs = [(qi, kj) for qi in range(nq) for kj in range(qi + 1)]
    store_scores(0, 0, 0)
    carry = None
    for idx, (qi, kj) in enumerate(pairs):
        slot = idx % 2
        if kj == 0:
            acc_ref[...] = jnp.zeros_like(acc_ref)
            carry = tuple((jnp.full((1, T), -jnp.inf, F32), jnp.zeros((1, T), F32)) for _ in range(heads))
        if idx + 1 < len(pairs):
            store_scores(*pairs[idx + 1], 1 - slot)
        carry = tuple(softmax_update(h, kj, slot, carry[h], kj == qi) for h in range(heads))
        if kj == qi:
            outs = [acc_ref[h] * (1.0 / carry[h][1]) for h in range(heads)]
            o = jnp.concatenate(outs, axis=0)
            o_ref[0, qi * T:(qi + 1) * T, :] = o.T.astype(o_ref.dtype)


def _attention(qt, k, vt):
    b, nh, nt, _, T = qt.shape
    s = nt * T
    hp = 2
    return pl.pallas_call(
        _attn_kernel,
        grid=(b, nh // hp),
        in_specs=[
            pl.BlockSpec((1, hp, nt, MLA_QK_PAD, T), lambda i, j: (i, j, 0, 0, 0)),
            pl.BlockSpec((1, hp, s, MLA_QK_PAD), lambda i, j: (i, j, 0, 0)),
            pl.BlockSpec((1, hp, nt, MLA_V_DIM, T), lambda i, j: (i, j, 0, 0, 0)),
        ],
        out_specs=pl.BlockSpec((1, s, hp * MLA_V_DIM), lambda i, j: (i, 0, j)),
        out_shape=jax.ShapeDtypeStruct((b, s, nh * MLA_V_DIM), BF16),
        scratch_shapes=[pltpu.VMEM((hp, MLA_V_DIM, T), F32), pltpu.VMEM((2, hp, T, T), F32)],
        compiler_params=_params(("parallel", "parallel")),
        name="attention",
    )(qt, k, vt)


def _mix_kernel(x_ref, ys_ref, at_ref, gt_ref, gb_ref, ws_ref, wm_ref, wo_ref, o_ref):
    y_ssd = jnp.dot(ys_ref[...], ws_ref[...], preferred_element_type=F32)
    y_mla = jnp.dot(at_ref[...], wm_ref[...], preferred_element_type=F32)
    g = _sigmoid(gt_ref[...].astype(F32) + gb_ref[...])
    mix = g[:, :D_MODEL] * y_ssd + g[:, D_MODEL:] * y_mla
    o_ref[...] = x_ref[...] + jnp.dot(mix.astype(BF16), wo_ref[...], preferred_element_type=F32)


def _mix(x2d, ys, at, gt, gb, ws, wm, wo, tm):
    t, d = x2d.shape
    tile = lambda n: pl.BlockSpec((tm, n), lambda i: (i, 0))
    const = lambda a: pl.BlockSpec(a.shape, lambda i: (0, 0))
    return pl.pallas_call(
        _mix_kernel,
        grid=(t // tm,),
        in_specs=[tile(d), tile(ys.shape[1]), tile(at.shape[1]), tile(gt.shape[1]),
                  const(gb), const(ws), const(wm), const(wo)],
        out_specs=tile(d),
        out_shape=jax.ShapeDtypeStruct((t, d), F32),
        compiler_params=_params(("parallel",)),
        name="mix",
    )(x2d, ys, at, gt, gb, ws, wm, wo)


def _ffn_kernel(x_ref, g_ref, wup_ref, cw_ref, cb_ref, wdn_ref, o_ref, tail_ref):
    tm = x_ref.shape[1]

    @pl.when(pl.program_id(1) == 0)
    def _():
        tail_ref[...] = jnp.zeros_like(tail_ref)

    x = x_ref[0]
    h = _rms_rows(x, g_ref[...]).astype(BF16)

    def up(c0):
        return jnp.dot(h, wup_ref[:, c0:c0 + FFN_COL_CHUNK], preferred_element_type=F32)

    def conv(u, c0):
        cs = slice(c0, c0 + FFN_COL_CHUNK)
        ext = jnp.concatenate([tail_ref[:, cs], u], axis=0)
        tail_ref[:, cs] = u[tm - SUBLANES:]
        cw = cw_ref[:, cs]
        out = cb_ref[:, cs] + ext[SUBLANES:] * cw[FFN_CONV - 1:FFN_CONV]
        for k in range(1, FFN_CONV):
            out = out + ext[SUBLANES - k:SUBLANES - k + tm] * cw[FFN_CONV - 1 - k:FFN_CONV - k]
        return out

    n_chunks = D_FF // FFN_COL_CHUNK
    acc = x
    u_next = (up(0), up(D_FF))
    for c in range(n_chunks):
        c0 = c * FFN_COL_CHUNK
        u_gate, u_val = u_next
        if c + 1 < n_chunks:
            u_next = (up(c0 + FFN_COL_CHUNK), up(D_FF + c0 + FFN_COL_CHUNK))
        act = (_silu(conv(u_gate, c0)) * conv(u_val, D_FF + c0)).astype(BF16)
        acc = acc + jnp.dot(act, wdn_ref[c0:c0 + FFN_COL_CHUNK, :], preferred_element_type=F32)
    o_ref[0] = acc


def _ffn(x, g, wup, cw, cb, wdn, tm):
    b, s, d = x.shape
    const = lambda a: pl.BlockSpec(a.shape, lambda i, j: (0, 0), pipeline_mode=pl.Buffered(1))
    return pl.pallas_call(
        _ffn_kernel,
        grid=(b, s // tm),
        in_specs=[pl.BlockSpec((1, tm, d), lambda i, j: (i, j, 0)),
                  const(g), const(wup), const(cw), const(cb), const(wdn)],
        out_specs=pl.BlockSpec((1, tm, d), lambda i, j: (i, j, 0)),
        out_shape=jax.ShapeDtypeStruct((b, s, d), F32),
        scratch_shapes=[pltpu.VMEM((SUBLANES, 2 * D_FF), F32)],
        compiler_params=_params(("parallel", "arbitrary")),
        name="ffn",
    )(x, g, wup, cw, cb, wdn)


def _rope_tables_t(seq):
    inv = 1.0 / (ROPE_THETA ** (jnp.arange(0, MLA_ROPE, 2, dtype=F32) / MLA_ROPE))
    ang = inv[:, None] * jnp.arange(seq, dtype=F32)[None, :]
    return jnp.cos(ang), jnp.sin(ang)


def _head_expand_matrix():
    r = jnp.arange(4 * SSD_HEADS)[:, None]
    c = jnp.arange(D_INNER)[None, :]
    return ((r < 3 * SSD_HEADS) & ((r % SSD_HEADS) == (c // SSD_HEAD_DIM))).astype(BF16)


def _layer(x, p):
    b, s, d = x.shape
    t = b * s
    x2d = x.reshape(t, d)
    w_in = p["w_in"]
    col = lambda a: a.reshape(-1, 1)
    rowv = lambda a: a.reshape(1, -1)
    g_mix = rowv(p["norm_mix_g"])

    w_big = jnp.concatenate([w_in[:, :OFF_XBC], w_in[:, OFF_KR:]], axis=1).astype(BF16)
    z, xbc, gates = _norm_proj(x2d, g_mix, w_big, (D_INNER, SSD_CONV_DIM, 2 * D_MODEL), min(512, t))

    ws_t = jnp.concatenate([w_in[:, OFF_DT:OFF_KVA], w_in[:, OFF_XBC:OFF_DT], w_in[:, OFF_KVA:OFF_KR]],
                           axis=1).T.astype(BF16)
    cos_t, sin_t = _rope_tables_t(s)
    qt, k, vt, dt_t = _mla_prep(
        x, g_mix, ws_t, col(p["q_a_norm_g"]), p["w_uq"].T.astype(BF16), col(p["kv_a_norm_g"]),
        p["w_ukv"].T.astype(BF16), col(p["q_norm_g"]), col(p["k_norm_g"]), cos_t, sin_t, min(512, s))

    y_ssd = _ssd(xbc.reshape(b, s, -1), z.reshape(b, s, -1), dt_t, p["conv_ssd_w"], rowv(p["conv_ssd_b"]),
                 col(p["dt_bias"]), col(p["a_log"]), rowv(jnp.repeat(p["d_skip"], SSD_HEAD_DIM)),
                 rowv(p["ssd_norm_g"]), _head_expand_matrix())
    attn = _attention(qt, k, vt)

    x1 = _mix(x2d, y_ssd.reshape(t, -1), attn.reshape(t, -1), gates, rowv(p["gate_b"]),
              p["w_ssd_proj"].astype(BF16), p["w_mla_proj"].astype(BF16), p["w_o"].astype(BF16), min(512, t))
    out = _ffn(x1.reshape(b, s, d), rowv(p["norm_ffn_g"]), p["w_up"].astype(BF16), p["conv_ffn_w"],
               rowv(p["conv_ffn_b"]), p["w_down"].astype(BF16), min(512, s))
    return out


def kernel(x, norm_mix_g, w_in, conv_ssd_w, conv_ssd_b, dt_bias, a_log, d_skip, ssd_norm_g, w_ssd_proj,
           q_a_norm_g, w_uq, kv_a_norm_g, w_ukv, q_norm_g, k_norm_g, w_mla_proj, gate_b, w_o, norm_ffn_g,
           w_up, conv_ffn_w, conv_ffn_b, w_down):
    params = dict(norm_mix_g=norm_mix_g, w_in=w_in, conv_ssd_w=conv_ssd_w, conv_ssd_b=conv_ssd_b,
                  dt_bias=dt_bias, a_log=a_log, d_skip=d_skip, ssd_norm_g=ssd_norm_g, w_ssd_proj=w_ssd_proj,
                  q_a_norm_g=q_a_norm_g, w_uq=w_uq, kv_a_norm_g=kv_a_norm_g, w_ukv=w_ukv, q_norm_g=q_norm_g,
                  k_norm_g=k_norm_g, w_mla_proj=w_mla_proj, gate_b=gate_b, w_o=w_o, norm_ffn_g=norm_ffn_g,
                  w_up=w_up, conv_ffn_w=conv_ffn_w, conv_ffn_b=conv_ffn_b, w_down=w_down)
    for i in range(w_in.shape[0]):
        x = _layer(x, {name: v[i] for name, v in params.items()})
    return x
```

```python
import functools

import jax
import jax.numpy as jnp
from jax import lax
from jax.experimental import pallas as pl
from jax.experimental.pallas import tpu as pltpu

F32 = jnp.float32
BF16 = jnp.bfloat16

D_MODEL = 1024
D_INNER = 2048
SSD_HEADS = 32
SSD_HEAD_DIM = 64
SSD_GROUPS = 4
SSD_STATE = 128
SSD_CONV = 4
SSD_CHUNK = 128
SSD_CONV_DIM = D_INNER + 2 * SSD_GROUPS * SSD_STATE
GROUP_WIDTH = D_INNER // SSD_GROUPS
MLA_HEADS = 16
MLA_Q_LORA = 256
MLA_KV_LORA = 128
MLA_NOPE = 64
MLA_ROPE = 32
MLA_QK_DIM = MLA_NOPE + MLA_ROPE
MLA_QK_PAD = 128
MLA_V_DIM = 64
MLA_V_ROWS = 80
ROPE_THETA = 10000.0
D_FF = 2816
FFN_CONV = 3
NORM_EPS = 1e-6
LOG2_E = 1.4426950408889634

OFF_Z = D_INNER
OFF_XBC = OFF_Z + SSD_CONV_DIM
OFF_DT = OFF_XBC + SSD_HEADS
OFF_QA = OFF_DT + MLA_Q_LORA
OFF_KVA = OFF_QA + MLA_KV_LORA
OFF_KR = OFF_KVA + MLA_ROPE

SUBLANES = 8
LANES = 128
FFN_BATCH = SUBLANES
FFN_TOKENS = 64
FFN_ROW_SPLIT = 2
BF16_ROWS = 16
ATTN_TILE = 512
FFN_COL_CHUNK = 256
PROJ_COL_CHUNK = 1024
VMEM_LIMIT = 56 * 1024 * 1024


def _params(sem, flags=None):
    return pltpu.CompilerParams(dimension_semantics=sem, vmem_limit_bytes=VMEM_LIMIT, flags=flags)


def _rms_rows(x, g):
    ms = jnp.mean(x * x, axis=-1, keepdims=True)
    return x * lax.rsqrt(ms + NORM_EPS) * g


def _rms_cols(x, g):
    ms = jnp.mean(x * x, axis=0, keepdims=True)
    return x * lax.rsqrt(ms + NORM_EPS) * g


def _sigmoid(x):
    return 1.0 / (1.0 + jnp.exp(-x))


def _silu(x):
    return x * _sigmoid(x)


def _norm_proj_kernel(x_ref, g_ref, w_ref, *o_refs):
    h = _rms_rows(x_ref[...], g_ref[...]).astype(BF16)
    c0 = 0
    for o_ref in o_refs:
        for j in range(o_ref.shape[1] // PROJ_COL_CHUNK):
            cs = slice(j * PROJ_COL_CHUNK, (j + 1) * PROJ_COL_CHUNK)
            ws = slice(c0 + cs.start, c0 + cs.stop)
            o_ref[:, cs] = jnp.dot(h, w_ref[:, ws], preferred_element_type=F32).astype(o_ref.dtype)
        c0 += o_ref.shape[1]


def _norm_proj(x2d, g, w, widths, tm):
    t, d = x2d.shape
    return pl.pallas_call(
        _norm_proj_kernel,
        grid=(t // tm,),
        in_specs=[
            pl.BlockSpec((tm, d), lambda i: (i, 0)),
            pl.BlockSpec((1, d), lambda i: (0, 0)),
            pl.BlockSpec(w.shape, lambda i: (0, 0), pipeline_mode=pl.Buffered(1)),
        ],
        out_specs=[pl.BlockSpec((tm, n), lambda i: (i, 0)) for n in widths],
        out_shape=[jax.ShapeDtypeStruct((t, n), BF16) for n in widths],
        compiler_params=_params(("parallel",)),
        name="norm_proj",
    )(x2d, g, w)


def _rope_rows(x1, x2, cos, sin):
    return x1 * cos - x2 * sin, x1 * sin + x2 * cos


def _mla_prep_kernel(x_ref, g_ref, ws_ref, qag_ref, wuq_ref, kvg_ref, wukv_ref, qg_ref, kg_ref,
                     cos_ref, sin_ref, qt_ref, k_ref, vt_ref, dt_ref):
    tm = x_ref.shape[1]
    h = _rms_rows(x_ref[0], g_ref[...]).astype(BF16)
    st = lax.dot_general(ws_ref[...], h, (((1,), (1,)), ((), ())), preferred_element_type=F32)
    qa = st[0:MLA_Q_LORA]
    kva = st[MLA_Q_LORA:MLA_Q_LORA + MLA_KV_LORA]
    o = MLA_Q_LORA + MLA_KV_LORA
    dt_ref[0] = st[o:o + SSD_HEADS]
    kr = st[o + SSD_HEADS:o + SSD_HEADS + MLA_ROPE]

    qa_n = _rms_cols(qa, qag_ref[...]).astype(BF16)
    kva_n = _rms_cols(kva, kvg_ref[...]).astype(BF16)
    q_all = jnp.dot(wuq_ref[...], qa_n, preferred_element_type=F32)
    kv_all = jnp.dot(wukv_ref[...], kva_n, preferred_element_type=F32)

    cos = cos_ref[...]
    sin = sin_ref[...]
    qg = qg_ref[...]
    kg = kg_ref[...]
    half = MLA_ROPE // 2
    scale = MLA_QK_DIM ** -0.5 * LOG2_E
    kr_ss = jnp.sum(kr * kr, axis=0, keepdims=True)
    pad = jnp.zeros((MLA_QK_PAD - MLA_QK_DIM, tm), F32)
    n_sub = tm // ATTN_TILE
    tail_row = lax.broadcasted_iota(jnp.int32, (MLA_V_ROWS - MLA_V_DIM, tm), 0)
    v_tail = (tail_row == 0).astype(F32)
    for hd in range(MLA_HEADS):
        qh = _rms_cols(q_all[MLA_QK_DIM * hd:MLA_QK_DIM * (hd + 1)], qg) * scale
        r1, r2 = _rope_rows(qh[MLA_NOPE:MLA_NOPE + half], qh[MLA_NOPE + half:], cos, sin)
        qfull = jnp.concatenate([qh[:MLA_NOPE], r1, r2, pad], axis=0).astype(BF16)

        base = (MLA_NOPE + MLA_V_DIM) * hd
        kn = kv_all[base:base + MLA_NOPE]
        v = jnp.concatenate([kv_all[base + MLA_NOPE:base + MLA_NOPE + MLA_V_DIM], v_tail], axis=0).astype(BF16)
        ss = (jnp.sum(kn * kn, axis=0, keepdims=True) + kr_ss) * (1.0 / MLA_QK_DIM)
        rs = lax.rsqrt(ss + NORM_EPS)
        kn = kn * rs * kg[:MLA_NOPE]
        krn = kr * rs * kg[MLA_NOPE:]
        r1, r2 = _rope_rows(krn[:half], krn[half:], cos, sin)
        kfull = jnp.concatenate([kn, r1, r2, pad], axis=0)
        k_ref[0, hd] = kfull.T.astype(BF16)
        for j in range(n_sub):
            sl = slice(j * ATTN_TILE, (j + 1) * ATTN_TILE)
            qt_ref[0, hd, j] = qfull[:, sl]
            vt_ref[0, hd, j] = v[:, sl]


def _mla_prep(x, g, ws_t, qag, wuq_t, kvg, wukv_t, qg, kg, cos_t, sin_t, tm):
    b, s, d = x.shape
    nt = s // ATTN_TILE
    n_sub = tm // ATTN_TILE
    const = lambda shape: pl.BlockSpec(shape, lambda i, j: (0,) * len(shape))
    return pl.pallas_call(
        _mla_prep_kernel,
        grid=(b, s // tm),
        in_specs=[
            pl.BlockSpec((1, tm, d), lambda i, j: (i, j, 0)),
            const(g.shape), const(ws_t.shape), const(qag.shape), const(wuq_t.shape),
            const(kvg.shape), const(wukv_t.shape), const(qg.shape), const(kg.shape),
            pl.BlockSpec((MLA_ROPE // 2, tm), lambda i, j: (0, j)),
            pl.BlockSpec((MLA_ROPE // 2, tm), lambda i, j: (0, j)),
        ],
        out_specs=[
            pl.BlockSpec((1, MLA_HEADS, n_sub, MLA_QK_PAD, ATTN_TILE), lambda i, j: (i, 0, j, 0, 0)),
            pl.BlockSpec((1, MLA_HEADS, tm, MLA_QK_PAD), lambda i, j: (i, 0, j, 0)),
            pl.BlockSpec((1, MLA_HEADS, n_sub, MLA_V_ROWS, ATTN_TILE), lambda i, j: (i, 0, j, 0, 0)),
            pl.BlockSpec((1, SSD_HEADS, tm), lambda i, j: (i, 0, j)),
        ],
        out_shape=[
            jax.ShapeDtypeStruct((b, MLA_HEADS, nt, MLA_QK_PAD, ATTN_TILE), BF16),
            jax.ShapeDtypeStruct((b, MLA_HEADS, s, MLA_QK_PAD), BF16),
            jax.ShapeDtypeStruct((b, MLA_HEADS, nt, MLA_V_ROWS, ATTN_TILE), BF16),
            jax.ShapeDtypeStruct((b, SSD_HEADS, s), F32),
        ],
        compiler_params=_params(("parallel", "parallel")),
        name="mla_prep",
    )(x, g, ws_t, qag, wuq_t, kvg, wukv_t, qg, kg, cos_t, sin_t)


def _split3_rows(v):
    hi = v.astype(BF16).astype(F32)
    r1 = v - hi
    mid = r1.astype(BF16).astype(F32)
    lo = (r1 - mid).astype(BF16).astype(F32)
    return jnp.concatenate([hi, mid, lo, jnp.zeros_like(v)], axis=0)


def _ssd_kernel(xbc_ref, z_ref, dt_ref, cw_ref, cb_ref, dtb_ref, alog_ref, dskip_ref, ng_ref, e3_ref, shift_ref,
                o_ref, xprev_ref, state_ref):
    L = SSD_CHUNK

    @pl.when(pl.program_id(1) == 0)
    def _():
        xprev_ref[...] = jnp.zeros_like(xprev_ref)
        state_ref[...] = jnp.zeros_like(state_ref)

    cur = xbc_ref[0]
    ext = jnp.concatenate([xprev_ref[...], cur], axis=0)
    xprev_ref[...] = cur[L - BF16_ROWS:]
    shifted = jnp.dot(shift_ref[...], ext, preferred_element_type=F32)
    cw = cw_ref[...]
    acc = cb_ref[...] + cur.astype(F32) * cw[SSD_CONV - 1:SSD_CONV]
    for k in range(1, SSD_CONV):
        acc = acc + shifted[(k - 1) * L:k * L] * cw[SSD_CONV - 1 - k:SSD_CONV - k]
    act = _silu(acc)
    xs = act[:, :D_INNER]
    bm = act[:, D_INNER:D_INNER + SSD_GROUPS * SSD_STATE]
    cm = act[:, D_INNER + SSD_GROUPS * SSD_STATE:]

    dt_raw = dt_ref[0] + dtb_ref[...]
    dt_t = jnp.maximum(dt_raw, 0.0) + jnp.log(1.0 + jnp.exp(-jnp.abs(dt_raw)))
    a_t = dt_t * (-LOG2_E * jnp.exp(alog_ref[...]))
    row = lax.broadcasted_iota(jnp.int32, (L, L), 0)
    col = lax.broadcasted_iota(jnp.int32, (L, L), 1)
    triu = (row <= col).astype(F32)
    acum_t = jnp.dot(a_t, triu, preferred_element_type=F32, precision=lax.Precision.HIGHEST)
    w2_t = dt_t * jnp.exp2(acum_t[:, L - 1:L] - acum_t)
    ea_t = jnp.exp2(acum_t)
    src_t = acum_t - jnp.log2(dt_t)

    def expand(v_t):
        p = _split3_rows(v_t).T.astype(BF16)
        return jnp.dot(p, e3_ref[...], preferred_element_type=F32)

    w2_x = expand(w2_t)
    ea_x = expand(ea_t)
    acum_c = jnp.concatenate([acum_t, jnp.zeros((L - SSD_HEADS, L), F32)], axis=0).T

    xw2 = (xs * w2_x).astype(BF16)
    causal = row >= col
    left = col < SSD_HEAD_DIM
    heads_per_group = SSD_HEADS // SSD_GROUPS
    y_parts = []
    for g in range(SSD_GROUPS):
        gs = slice(g * GROUP_WIDTH, (g + 1) * GROUP_WIDTH)
        b_g = bm[:, g * SSD_STATE:(g + 1) * SSD_STATE]
        c_g = cm[:, g * SSD_STATE:(g + 1) * SSD_STATE].astype(BF16)
        scores = lax.dot_general(c_g, b_g.astype(BF16), (((1,), (1,)), ((), ())),
                                 preferred_element_type=F32)
        prev = state_ref[g]
        y_g = jnp.dot(c_g, prev.astype(BF16), preferred_element_type=F32) * ea_x[:, gs]
        diag = []
        for pair in range(heads_per_group // 2):
            ms = []
            for e in range(2):
                hd = g * heads_per_group + 2 * pair + e
                seg = acum_c[:, hd:hd + 1] - src_t[hd:hd + 1, :]
                ms.append(scores * jnp.exp2(jnp.where(causal, seg, -jnp.inf)))
            m2 = jnp.concatenate(ms, axis=1).astype(BF16)
            c0 = g * GROUP_WIDTH + pair * 2 * SSD_HEAD_DIM
            x2 = xs[:, c0:c0 + 2 * SSD_HEAD_DIM]
            r2 = jnp.concatenate([jnp.where(left, x2, 0.0), jnp.where(left, 0.0, x2)], axis=0).astype(BF16)
            diag.append(jnp.dot(m2, r2, preferred_element_type=F32))
        y_parts.append(y_g + jnp.concatenate(diag, axis=1))
        new_state = jnp.dot(b_g.T.astype(BF16), xw2[:, gs], preferred_element_type=F32)
        state_ref[g] = prev * ea_x[L - 1:L, gs] + new_state

    y = jnp.concatenate(y_parts, axis=1) + xs * dskip_ref[...]
    y = y * _silu(z_ref[0].astype(F32))
    ng = ng_ref[...]
    outs = []
    for g in range(SSD_GROUPS):
        gs = slice(g * GROUP_WIDTH, (g + 1) * GROUP_WIDTH)
        outs.append(_rms_rows(y[:, gs], ng[:, gs]))
    o_ref[0] = jnp.concatenate(outs, axis=1).astype(o_ref.dtype)


def _conv_shift_matrix():
    r = jnp.arange((SSD_CONV - 1) * SSD_CHUNK)[:, None]
    c = jnp.arange(BF16_ROWS + SSD_CHUNK)[None, :]
    return (c == BF16_ROWS + (r % SSD_CHUNK) - (r // SSD_CHUNK + 1)).astype(BF16)


def _ssd(xbc, z, dt_t, cw, cb, dtb, alog, dskip_x, ng, e3):
    b, s, _ = xbc.shape
    shift = _conv_shift_matrix()
    L = SSD_CHUNK
    const = lambda shape: pl.BlockSpec(shape, lambda i, j: (0,) * len(shape))
    return pl.pallas_call(
        _ssd_kernel,
        grid=(b, s // L),
        in_specs=[
            pl.BlockSpec((1, L, SSD_CONV_DIM), lambda i, j: (i, j, 0)),
            pl.BlockSpec((1, L, D_INNER), lambda i, j: (i, j, 0)),
            pl.BlockSpec((1, SSD_HEADS, L), lambda i, j: (i, 0, j)),
            const(cw.shape), const(cb.shape), const(dtb.shape), const(alog.shape),
            const(dskip_x.shape), const(ng.shape), const(e3.shape), const(shift.shape),
        ],
        out_specs=pl.BlockSpec((1, L, D_INNER), lambda i, j: (i, j, 0)),
        out_shape=jax.ShapeDtypeStruct((b, s, D_INNER), BF16),
        scratch_shapes=[
            pltpu.VMEM((BF16_ROWS, SSD_CONV_DIM), BF16),
            pltpu.VMEM((SSD_GROUPS, SSD_STATE, GROUP_WIDTH), F32),
        ],
        compiler_params=_params(("parallel", "arbitrary")),
        name="ssd",
    )(xbc, z, dt_t, cw, cb, dtb, alog, dskip_x, ng, e3, shift)


def _attn_kernel(qt_ref, k_ref, vt_ref, o_ref, acc_ref, s_ref):
    nq = qt_ref.shape[2]
    heads = qt_ref.shape[1]
    T = ATTN_TILE
    row = lax.broadcasted_iota(jnp.int32, (T, T), 0)
    col = lax.broadcasted_iota(jnp.int32, (T, T), 1)
    diag_ok = row <= col

    def store_scores(qi, kj, slot):
        for h in range(heads):
            kt = k_ref[0, h, kj * T:(kj + 1) * T, :]
            s_ref[slot, h] = jnp.dot(kt, qt_ref[0, h, qi], preferred_element_type=F32)

    def softmax_update(h, kj, slot, m, masked):
        s = s_ref[slot, h]
        if masked:
            s = jnp.where(diag_ok, s, -jnp.inf)
        m_new = jnp.maximum(m, jnp.max(s, axis=0, keepdims=True))
        p = jnp.exp2(s - m_new).astype(BF16)
        pv = jnp.dot(vt_ref[0, h, kj], p, preferred_element_type=F32)
        acc_ref[h] = jnp.exp2(m - m_new) * acc_ref[h] + pv
        return m_new

    pairs = [(qi, kj) for qi in range(nq) for kj in range(qi + 1)]
    store_scores(0, 0, 0)
    carry = None
    for idx, (qi, kj) in enumerate(pairs):
        slot = idx % 2
        if kj == 0:
            acc_ref[...] = jnp.zeros_like(acc_ref)
            carry = tuple(jnp.full((1, T), -jnp.inf, F32) for _ in range(heads))
        if idx + 1 < len(pairs):
            store_scores(*pairs[idx + 1], 1 - slot)
        carry = tuple(softmax_update(h, kj, slot, carry[h], kj == qi) for h in range(heads))
        if kj == qi:
            outs = [acc_ref[h, :MLA_V_DIM] * (1.0 / acc_ref[h, MLA_V_DIM:MLA_V_DIM + 1]) for h in range(heads)]
            o = jnp.concatenate(outs, axis=0)
            o_ref[0, qi * T:(qi + 1) * T, :] = o.T.astype(o_ref.dtype)


def _attention(qt, k, vt):
    b, nh, nt, _, T = qt.shape
    s = nt * T
    hp = 2
    return pl.pallas_call(
        _attn_kernel,
        grid=(b, nh // hp),
        in_specs=[
            pl.BlockSpec((1, hp, nt, MLA_QK_PAD, T), lambda i, j: (i, j, 0, 0, 0)),
            pl.BlockSpec((1, hp, s, MLA_QK_PAD), lambda i, j: (i, j, 0, 0)),
            pl.BlockSpec((1, hp, nt, MLA_V_ROWS, T), lambda i, j: (i, j, 0, 0, 0)),
        ],
        out_specs=pl.BlockSpec((1, s, hp * MLA_V_DIM), lambda i, j: (i, 0, j)),
        out_shape=jax.ShapeDtypeStruct((b, s, nh * MLA_V_DIM), BF16),
        scratch_shapes=[pltpu.VMEM((hp, MLA_V_ROWS, T), F32), pltpu.VMEM((2, hp, T, T), F32)],
        compiler_params=_params(("parallel", "parallel")),
        name="attention",
    )(qt, k, vt)


def _mix_kernel(x_ref, ys_ref, at_ref, gt_ref, gb_ref, ws_ref, wm_ref, wo_ref, o_ref):
    y_ssd = jnp.dot(ys_ref[...], ws_ref[...], preferred_element_type=F32)
    y_mla = jnp.dot(at_ref[...], wm_ref[...], preferred_element_type=F32)
    g = _sigmoid(gt_ref[...].astype(F32) + gb_ref[...])
    mix = g[:, :D_MODEL] * y_ssd + g[:, D_MODEL:] * y_mla
    o_ref[...] = x_ref[...] + jnp.dot(mix.astype(BF16), wo_ref[...], preferred_element_type=F32)


def _mix(x2d, ys, at, gt, gb, ws, wm, wo, tm):
    t, d = x2d.shape
    tile = lambda n: pl.BlockSpec((tm, n), lambda i: (i, 0))
    const = lambda a: pl.BlockSpec(a.shape, lambda i: (0, 0))
    return pl.pallas_call(
        _mix_kernel,
        grid=(t // tm,),
        in_specs=[tile(d), tile(ys.shape[1]), tile(at.shape[1]), tile(gt.shape[1]),
                  const(gb), const(ws), const(wm), const(wo)],
        out_specs=tile(d),
        out_shape=jax.ShapeDtypeStruct((t, d), F32),
        compiler_params=_params(("parallel",)),
        name="mix",
    )(x2d, ys, at, gt, gb, ws, wm, wo)


def _ffn_kernel(x_ref, g_ref, wup_ref, cw_ref, cb_ref, wdn_ref, o_ref, tail_ref, perm_ref):
    nb, nt, d = x_ref.shape
    tm = nb * nt
    n_lane = d // LANES

    @pl.when(pl.program_id(1) == 0)
    def _():
        tail_ref[...] = jnp.zeros_like(tail_ref)

    for c in range(n_lane):
        for j in range(nb):
            perm_ref[c, pl.ds(j, nt, stride=nb), :] = x_ref[j, :, c * LANES:(c + 1) * LANES]
    x = jnp.concatenate([perm_ref[c] for c in range(n_lane)], axis=1)
    h = _rms_rows(x, g_ref[...]).astype(BF16)

    keep = (FFN_CONV - 1) * nb
    rows = tm // FFN_ROW_SPLIT
    h_parts = [h[r * rows:(r + 1) * rows] for r in range(FFN_ROW_SPLIT)]

    def conv(us, c0):
        cs = slice(c0, c0 + FFN_COL_CHUNK)
        cw = cw_ref[:, cs]
        prev = tail_ref[:, cs]
        tail_ref[:, cs] = us[-1][rows - keep:]
        outs = []
        for u in us:
            ext = jnp.concatenate([prev, u], axis=0)
            out = cb_ref[:, cs] + u * cw[FFN_CONV - 1:FFN_CONV]
            for k in range(1, FFN_CONV):
                out = out + ext[keep - k * nb:keep - k * nb + rows] * cw[FFN_CONV - 1 - k:FFN_CONV - k]
            outs.append(out)
            prev = u[rows - keep:]
        return outs

    def up(col):
        return [jnp.dot(hp, wup_ref[:, col:col + FFN_COL_CHUNK], preferred_element_type=F32) for hp in h_parts]

    n_chunks = D_FF // FFN_COL_CHUNK
    accs = [x[r * rows:(r + 1) * rows] for r in range(FFN_ROW_SPLIT)]
    u_next = (up(0), up(D_FF))
    for c in range(n_chunks):
        c0 = c * FFN_COL_CHUNK
        u_gate, u_val = u_next
        if c + 1 < n_chunks:
            u_next = (up(c0 + FFN_COL_CHUNK), up(D_FF + c0 + FFN_COL_CHUNK))
        gate = conv(u_gate, c0)
        val = conv(u_val, D_FF + c0)
        for r in range(FFN_ROW_SPLIT):
            act = (_silu(gate[r]) * val[r]).astype(BF16)
            accs[r] = accs[r] + jnp.dot(act, wdn_ref[c0:c0 + FFN_COL_CHUNK, :], preferred_element_type=F32)
    acc = jnp.concatenate(accs, axis=0)

    for c in range(n_lane):
        perm_ref[c] = acc[:, c * LANES:(c + 1) * LANES]
    for c in range(n_lane):
        for j in range(nb):
            o_ref[j, :, c * LANES:(c + 1) * LANES] = perm_ref[c, pl.ds(j, nt, stride=nb), :]


def _ffn(x, g, wup, cw, cb, wdn):
    b, s, d = x.shape
    nb, nt = FFN_BATCH, FFN_TOKENS
    const = lambda a: pl.BlockSpec(a.shape, lambda i, j: (0, 0), pipeline_mode=pl.Buffered(1))
    return pl.pallas_call(
        _ffn_kernel,
        grid=(b // nb, s // nt),
        in_specs=[pl.BlockSpec((nb, nt, d), lambda i, j: (i, j, 0)),
                  const(g), const(wup), const(cw), const(cb), const(wdn)],
        out_specs=pl.BlockSpec((nb, nt, d), lambda i, j: (i, j, 0)),
        out_shape=jax.ShapeDtypeStruct((b, s, d), F32),
        scratch_shapes=[pltpu.VMEM(((FFN_CONV - 1) * nb, 2 * D_FF), F32),
                        pltpu.VMEM((d // LANES, nb * nt, LANES), F32)],
        compiler_params=_params(("parallel", "arbitrary")),
        name="ffn",
    )(x, g, wup, cw, cb, wdn)


def _rope_tables_t(seq):
    inv = 1.0 / (ROPE_THETA ** (jnp.arange(0, MLA_ROPE, 2, dtype=F32) / MLA_ROPE))
    ang = inv[:, None] * jnp.arange(seq, dtype=F32)[None, :]
    return jnp.cos(ang), jnp.sin(ang)


def _head_expand_matrix():
    r = jnp.arange(4 * SSD_HEADS)[:, None]
    c = jnp.arange(D_INNER)[None, :]
    return ((r < 3 * SSD_HEADS) & ((r % SSD_HEADS) == (c // SSD_HEAD_DIM))).astype(BF16)


def _layer(x, p):
    b, s, d = x.shape
    t = b * s
    x2d = x.reshape(t, d)
    w_in = p["w_in"]
    col = lambda a: a.reshape(-1, 1)
    rowv = lambda a: a.reshape(1, -1)
    g_mix = rowv(p["norm_mix_g"])

    w_big = jnp.concatenate([w_in[:, :OFF_XBC], w_in[:, OFF_KR:]], axis=1).astype(BF16)
    z, xbc, gates = _norm_proj(x2d, g_mix, w_big, (D_INNER, SSD_CONV_DIM, 2 * D_MODEL), min(512, t))

    ws_t = jnp.concatenate([w_in[:, OFF_DT:OFF_KVA], w_in[:, OFF_XBC:OFF_DT], w_in[:, OFF_KVA:OFF_KR]],
                           axis=1).T.astype(BF16)
    cos_t, sin_t = _rope_tables_t(s)
    qt, k, vt, dt_t = _mla_prep(
        x, g_mix, ws_t, col(p["q_a_norm_g"]), p["w_uq"].T.astype(BF16), col(p["kv_a_norm_g"]),
        p["w_ukv"].T.astype(BF16), col(p["q_norm_g"]), col(p["k_norm_g"]), cos_t, sin_t, min(512, s))

    y_ssd = _ssd(xbc.reshape(b, s, -1), z.reshape(b, s, -1), dt_t, p["conv_ssd_w"], rowv(p["conv_ssd_b"]),
                 col(p["dt_bias"]), col(p["a_log"]), rowv(jnp.repeat(p["d_skip"], SSD_HEAD_DIM)),
                 rowv(p["ssd_norm_g"]), _head_expand_matrix())
    attn = _attention(qt, k, vt)

    x1 = _mix(x2d, y_ssd.reshape(t, -1), attn.reshape(t, -1), gates, rowv(p["gate_b"]),
              p["w_ssd_proj"].astype(BF16), p["w_mla_proj"].astype(BF16), p["w_o"].astype(BF16), min(512, t))
    out = _ffn(x1.reshape(b, s, d), rowv(p["norm_ffn_g"]), p["w_up"].astype(BF16), p["conv_ffn_w"],
               rowv(p["conv_ffn_b"]), p["w_down"].astype(BF16))
    return out


def kernel(x, norm_mix_g, w_in, conv_ssd_w, conv_ssd_b, dt_bias, a_log, d_skip, ssd_norm_g, w_ssd_proj,
           q_a_norm_g, w_uq, kv_a_norm_g, w_ukv, q_norm_g, k_norm_g, w_mla_proj, gate_b, w_o, norm_ffn_g,
           w_up, conv_ffn_w, conv_ffn_b, w_down):
    params = dict(norm_mix_g=norm_mix_g, w_in=w_in, conv_ssd_w=conv_ssd_w, conv_ssd_b=conv_ssd_b,
                  dt_bias=dt_bias, a_log=a_log, d_skip=d_skip, ssd_norm_g=ssd_norm_g, w_ssd_proj=w_ssd_proj,
                  q_a_norm_g=q_a_norm_g, w_uq=w_uq, kv_a_norm_g=kv_a_norm_g, w_ukv=w_ukv, q_norm_g=q_norm_g,
                  k_norm_g=k_norm_g, w_mla_proj=w_mla_proj, gate_b=gate_b, w_o=w_o, norm_ffn_g=norm_ffn_g,
                  w_up=w_up, conv_ffn_w=conv_ffn_w, conv_ffn_b=conv_ffn_b, w_down=w_down)
    for i in range(w_in.shape[0]):
        x = _layer(x, {name: v[i] for name, v in params.items()})
    return x
```

```python
import functools

import jax
import jax.numpy as jnp
from jax import lax
from jax.experimental import pallas as pl
from jax.experimental.pallas import tpu as pltpu

F32 = jnp.float32
BF16 = jnp.bfloat16

D_MODEL = 1024
D_INNER = 2048
SSD_HEADS = 32
SSD_HEAD_DIM = 64
SSD_GROUPS = 4
SSD_STATE = 128
SSD_CONV = 4
SSD_CHUNK = 128
SSD_CONV_DIM = D_INNER + 2 * SSD_GROUPS * SSD_STATE
GROUP_WIDTH = D_INNER // SSD_GROUPS
MLA_HEADS = 16
MLA_Q_LORA = 256
MLA_KV_LORA = 128
MLA_NOPE = 64
MLA_ROPE = 32
MLA_QK_DIM = MLA_NOPE + MLA_ROPE
MLA_QK_PAD = 128
MLA_V_DIM = 64
MLA_V_ROWS = 80
ROPE_THETA = 10000.0
D_FF = 2816
FFN_CONV = 3
NORM_EPS = 1e-6
LOG2_E = 1.4426950408889634

OFF_Z = D_INNER
OFF_XBC = OFF_Z + SSD_CONV_DIM
OFF_DT = OFF_XBC + SSD_HEADS
OFF_QA = OFF_DT + MLA_Q_LORA
OFF_KVA = OFF_QA + MLA_KV_LORA
OFF_KR = OFF_KVA + MLA_ROPE

SUBLANES = 8
LANES = 128
TOKMAJ_BATCH = SUBLANES
TOKMAJ_TOKENS = 64
FFN_BATCH = TOKMAJ_BATCH
FFN_TOKENS = TOKMAJ_TOKENS
PROJ_ROW_SPLIT = 2
FFN_ROW_SPLIT = 2
BF16_ROWS = 16
ATTN_TILE = 512
FFN_COL_CHUNK = 256
PROJ_COL_CHUNK = 256
VMEM_LIMIT = 56 * 1024 * 1024


def _params(sem, flags=None):
    return pltpu.CompilerParams(dimension_semantics=sem, vmem_limit_bytes=VMEM_LIMIT, flags=flags)


def _rms_rows(x, g):
    ms = jnp.mean(x * x, axis=-1, keepdims=True)
    return x * lax.rsqrt(ms + NORM_EPS) * g


def _rms_cols(x, g):
    ms = jnp.mean(x * x, axis=0, keepdims=True)
    return x * lax.rsqrt(ms + NORM_EPS) * g


def _sigmoid(x):
    return 1.0 / (1.0 + jnp.exp(-x))


def _silu(x):
    return x * _sigmoid(x)


def _to_token_major(x_ref, perm_ref):
    nb, nt, d = x_ref.shape
    for c in range(d // LANES):
        for j in range(nb):
            perm_ref[c, pl.ds(j, nt, stride=nb), :] = x_ref[j, :, c * LANES:(c + 1) * LANES]
    return jnp.concatenate([perm_ref[c] for c in range(d // LANES)], axis=1)


def _from_token_major(val, o_ref, c0, perm_ref):
    nb, nt, _ = o_ref.shape
    n_lane = val.shape[1] // LANES
    for c in range(n_lane):
        perm_ref[c] = val[:, c * LANES:(c + 1) * LANES]
    for c in range(n_lane):
        for j in range(nb):
            o_ref[j, :, c0 + c * LANES:c0 + (c + 1) * LANES] = perm_ref[c, pl.ds(j, nt, stride=nb), :].astype(o_ref.dtype)


def _norm_proj_kernel(x_ref, g_ref, w_ref, cw_ref, cb_ref, z_ref, xbc_ref, gates_ref, tail_ref, pin_ref, pout_ref):
    nb, nt, _ = x_ref.shape
    tm = nb * nt
    keep = (SSD_CONV - 1) * nb

    @pl.when(pl.program_id(1) == 0)
    def _():
        tail_ref[...] = jnp.zeros_like(tail_ref)

    g = g_ref[...]
    h_std = _rms_rows(x_ref[...].reshape(tm, x_ref.shape[2]), g).astype(BF16)

    def proj_std(c0):
        u = jnp.dot(h_std, w_ref[:, c0:c0 + PROJ_COL_CHUNK], preferred_element_type=F32)
        return u.reshape(nb, nt, PROJ_COL_CHUNK)

    def z_chunk(j):
        cs = slice(j * PROJ_COL_CHUNK, (j + 1) * PROJ_COL_CHUNK)
        z_ref[:, :, cs] = _silu(proj_std(cs.start)).astype(BF16)

    def gates_chunk(j):
        cs = slice(j * PROJ_COL_CHUNK, (j + 1) * PROJ_COL_CHUNK)
        gates_ref[:, :, cs] = proj_std(OFF_XBC + cs.start).astype(BF16)

    h = _rms_rows(_to_token_major(x_ref, pin_ref), g).astype(BF16)
    rows = tm // PROJ_ROW_SPLIT
    h_parts = [h[r * rows:(r + 1) * rows] for r in range(PROJ_ROW_SPLIT)]

    def proj(c0):
        return [jnp.dot(hp, w_ref[:, c0:c0 + PROJ_COL_CHUNK], preferred_element_type=F32) for hp in h_parts]

    def xbc_chunk(j):
        cs = slice(j * PROJ_COL_CHUNK, (j + 1) * PROJ_COL_CHUNK)
        us = proj(OFF_Z + cs.start)
        cw = cw_ref[:, cs]
        prev = tail_ref[:, cs]
        tail_ref[:, cs] = us[-1][rows - keep:]
        outs = []
        for u in us:
            ext = jnp.concatenate([prev, u], axis=0)
            acc = cb_ref[:, cs] + u * cw[SSD_CONV - 1:SSD_CONV]
            for k in range(1, SSD_CONV):
                acc = acc + ext[keep - k * nb:keep - k * nb + rows] * cw[SSD_CONV - 1 - k:SSD_CONV - k]
            outs.append(_silu(acc))
            prev = u[rows - keep:]
        _from_token_major(jnp.concatenate(outs, axis=0), xbc_ref, cs.start, pout_ref.at[j % 2])

    plain = [functools.partial(z_chunk, j) for j in range(D_INNER // PROJ_COL_CHUNK)]
    plain += [functools.partial(gates_chunk, j) for j in range(2 * D_MODEL // PROJ_COL_CHUNK)]
    n_x = SSD_CONV_DIM // PROJ_COL_CHUNK
    for j in range(max(n_x, len(plain))):
        if j < n_x:
            xbc_chunk(j)
        if j < len(plain):
            plain[j]()


def _norm_proj(x, g, w, cw, cb):
    b, s, d = x.shape
    nb, nt = TOKMAJ_BATCH, TOKMAJ_TOKENS
    widths = (D_INNER, SSD_CONV_DIM, 2 * D_MODEL)
    const = lambda a: pl.BlockSpec(a.shape, lambda i, j: (0, 0))
    return pl.pallas_call(
        _norm_proj_kernel,
        grid=(b // nb, s // nt),
        in_specs=[
            pl.BlockSpec((nb, nt, d), lambda i, j: (i, j, 0)),
            const(g),
            pl.BlockSpec(w.shape, lambda i, j: (0, 0), pipeline_mode=pl.Buffered(1)),
            const(cw), const(cb),
        ],
        out_specs=[pl.BlockSpec((nb, nt, n), lambda i, j: (i, j, 0)) for n in widths],
        out_shape=[jax.ShapeDtypeStruct((b, s, n), BF16) for n in widths],
        scratch_shapes=[pltpu.VMEM(((SSD_CONV - 1) * nb, SSD_CONV_DIM), F32),
                        pltpu.VMEM((d // LANES, nb * nt, LANES), F32),
                        pltpu.VMEM((2, PROJ_COL_CHUNK // LANES, nb * nt, LANES), F32)],
        compiler_params=_params(("parallel", "arbitrary")),
        name="norm_proj",
    )(x, g, w, cw, cb)


def _rope_rows(x1, x2, cos, sin):
    return x1 * cos - x2 * sin, x1 * sin + x2 * cos


def _mla_prep_kernel(x_ref, g_ref, ws_ref, qag_ref, wuq_ref, kvg_ref, wukv_ref, qg_ref, kg_ref,
                     cos_ref, sin_ref, qt_ref, k_ref, vt_ref, dt_ref):
    tm = x_ref.shape[1]
    h = _rms_rows(x_ref[0], g_ref[...]).astype(BF16)
    st = lax.dot_general(ws_ref[...], h, (((1,), (1,)), ((), ())), preferred_element_type=F32)
    qa = st[0:MLA_Q_LORA]
    kva = st[MLA_Q_LORA:MLA_Q_LORA + MLA_KV_LORA]
    o = MLA_Q_LORA + MLA_KV_LORA
    dt_ref[0] = st[o:o + SSD_HEADS]
    kr = st[o + SSD_HEADS:o + SSD_HEADS + MLA_ROPE]

    qa_n = _rms_cols(qa, qag_ref[...]).astype(BF16)
    kva_n = _rms_cols(kva, kvg_ref[...]).astype(BF16)
    q_all = jnp.dot(wuq_ref[...], qa_n, preferred_element_type=F32)
    kv_all = jnp.dot(wukv_ref[...], kva_n, preferred_element_type=F32)

    cos = cos_ref[...]
    sin = sin_ref[...]
    qg = qg_ref[...]
    kg = kg_ref[...]
    half = MLA_ROPE // 2
    scale = MLA_QK_DIM ** -0.5 * LOG2_E
    kr_ss = jnp.sum(kr * kr, axis=0, keepdims=True)
    pad = jnp.zeros((MLA_QK_PAD - MLA_QK_DIM, tm), F32)
    n_sub = tm // ATTN_TILE
    tail_row = lax.broadcasted_iota(jnp.int32, (MLA_V_ROWS - MLA_V_DIM, tm), 0)
    v_tail = (tail_row == 0).astype(F32)
    for hd in range(MLA_HEADS):
        qh = _rms_cols(q_all[MLA_QK_DIM * hd:MLA_QK_DIM * (hd + 1)], qg) * scale
        r1, r2 = _rope_rows(qh[MLA_NOPE:MLA_NOPE + half], qh[MLA_NOPE + half:], cos, sin)
        qfull = jnp.concatenate([qh[:MLA_NOPE], r1, r2, pad], axis=0).astype(BF16)

        base = (MLA_NOPE + MLA_V_DIM) * hd
        kn = kv_all[base:base + MLA_NOPE]
        v = jnp.concatenate([kv_all[base + MLA_NOPE:base + MLA_NOPE + MLA_V_DIM], v_tail], axis=0).astype(BF16)
        ss = (jnp.sum(kn * kn, axis=0, keepdims=True) + kr_ss) * (1.0 / MLA_QK_DIM)
        rs = lax.rsqrt(ss + NORM_EPS)
        kn = kn * rs * kg[:MLA_NOPE]
        krn = kr * rs * kg[MLA_NOPE:]
        r1, r2 = _rope_rows(krn[:half], krn[half:], cos, sin)
        kfull = jnp.concatenate([kn, r1, r2, pad], axis=0)
        k_ref[0, hd] = kfull.T.astype(BF16)
        for j in range(n_sub):
            sl = slice(j * ATTN_TILE, (j + 1) * ATTN_TILE)
            qt_ref[0, hd, j] = qfull[:, sl]
            vt_ref[0, hd, j] = v[:, sl]


def _mla_prep(x, g, ws_t, qag, wuq_t, kvg, wukv_t, qg, kg, cos_t, sin_t, tm):
    b, s, d = x.shape
    nt = s // ATTN_TILE
    n_sub = tm // ATTN_TILE
    const = lambda shape: pl.BlockSpec(shape, lambda i, j: (0,) * len(shape))
    return pl.pallas_call(
        _mla_prep_kernel,
        grid=(b, s // tm),
        in_specs=[
            pl.BlockSpec((1, tm, d), lambda i, j: (i, j, 0)),
            const(g.shape), const(ws_t.shape), const(qag.shape), const(wuq_t.shape),
            const(kvg.shape), const(wukv_t.shape), const(qg.shape), const(kg.shape),
            pl.BlockSpec((MLA_ROPE // 2, tm), lambda i, j: (0, j)),
            pl.BlockSpec((MLA_ROPE // 2, tm), lambda i, j: (0, j)),
        ],
        out_specs=[
            pl.BlockSpec((1, MLA_HEADS, n_sub, MLA_QK_PAD, ATTN_TILE), lambda i, j: (i, 0, j, 0, 0)),
            pl.BlockSpec((1, MLA_HEADS, tm, MLA_QK_PAD), lambda i, j: (i, 0, j, 0)),
            pl.BlockSpec((1, MLA_HEADS, n_sub, MLA_V_ROWS, ATTN_TILE), lambda i, j: (i, 0, j, 0, 0)),
            pl.BlockSpec((1, SSD_HEADS, tm), lambda i, j: (i, 0, j)),
        ],
        out_shape=[
            jax.ShapeDtypeStruct((b, MLA_HEADS, nt, MLA_QK_PAD, ATTN_TILE), BF16),
            jax.ShapeDtypeStruct((b, MLA_HEADS, s, MLA_QK_PAD), BF16),
            jax.ShapeDtypeStruct((b, MLA_HEADS, nt, MLA_V_ROWS, ATTN_TILE), BF16),
            jax.ShapeDtypeStruct((b, SSD_HEADS, s), F32),
        ],
        compiler_params=_params(("parallel", "parallel")),
        name="mla_prep",
    )(x, g, ws_t, qag, wuq_t, kvg, wukv_t, qg, kg, cos_t, sin_t)


def _split3_rows(v):
    hi = v.astype(BF16).astype(F32)
    r1 = v - hi
    mid = r1.astype(BF16).astype(F32)
    lo = (r1 - mid).astype(BF16).astype(F32)
    return jnp.concatenate([hi, mid, lo, jnp.zeros_like(v)], axis=0)


def _ssd_kernel(xbc_ref, z_ref, dt_ref, dtb_ref, alog_ref, dskip_ref, ng_ref, e3_ref, o_ref, state_ref):
    L = SSD_CHUNK

    @pl.when(pl.program_id(1) == 0)
    def _():
        state_ref[...] = jnp.zeros_like(state_ref)

    xs_b = xbc_ref[0, :, :D_INNER]
    xs = xs_b.astype(F32)
    bm = xbc_ref[0, :, D_INNER:D_INNER + SSD_GROUPS * SSD_STATE]
    cm = xbc_ref[0, :, D_INNER + SSD_GROUPS * SSD_STATE:]

    dt_raw = dt_ref[0] + dtb_ref[...]
    dt_t = jnp.maximum(dt_raw, 0.0) + jnp.log(1.0 + jnp.exp(-jnp.abs(dt_raw)))
    a_t = dt_t * (-LOG2_E * jnp.exp(alog_ref[...]))
    row = lax.broadcasted_iota(jnp.int32, (L, L), 0)
    col = lax.broadcasted_iota(jnp.int32, (L, L), 1)
    triu = (row <= col).astype(F32)
    acum_t = jnp.dot(a_t, triu, preferred_element_type=F32, precision=lax.Precision.HIGHEST)
    w2_t = dt_t * jnp.exp2(acum_t[:, L - 1:L] - acum_t)
    ea_t = jnp.exp2(acum_t)
    src_t = acum_t - jnp.log2(dt_t)

    def expand(v_t):
        p = _split3_rows(v_t).T.astype(BF16)
        return jnp.dot(p, e3_ref[...], preferred_element_type=F32)

    w2_x = expand(w2_t)
    ea_x = expand(ea_t)
    acum_c = jnp.concatenate([acum_t, jnp.zeros((L - SSD_HEADS, L), F32)], axis=0).T

    xw2 = (xs * w2_x).astype(BF16)
    causal = row >= col
    left = (col < SSD_HEAD_DIM).astype(BF16)
    right = (col >= SSD_HEAD_DIM).astype(BF16)
    heads_per_group = SSD_HEADS // SSD_GROUPS
    y_parts = []
    for g in range(SSD_GROUPS):
        gs = slice(g * GROUP_WIDTH, (g + 1) * GROUP_WIDTH)
        b_g = bm[:, g * SSD_STATE:(g + 1) * SSD_STATE]
        c_g = cm[:, g * SSD_STATE:(g + 1) * SSD_STATE]
        scores = lax.dot_general(c_g, b_g, (((1,), (1,)), ((), ())), preferred_element_type=F32)
        prev = state_ref[g]
        y_g = jnp.dot(c_g, prev.astype(BF16), preferred_element_type=F32) * ea_x[:, gs]
        diag = []
        for pair in range(heads_per_group // 2):
            ms = []
            for e in range(2):
                hd = g * heads_per_group + 2 * pair + e
                seg = acum_c[:, hd:hd + 1] - src_t[hd:hd + 1, :]
                ms.append(scores * jnp.exp2(jnp.where(causal, seg, -jnp.inf)))
            m2 = jnp.concatenate(ms, axis=1).astype(BF16)
            c0 = g * GROUP_WIDTH + pair * 2 * SSD_HEAD_DIM
            x2 = xs_b[:, c0:c0 + 2 * SSD_HEAD_DIM]
            r2 = jnp.concatenate([x2 * left, x2 * right], axis=0)
            diag.append(jnp.dot(m2, r2, preferred_element_type=F32))
        y_parts.append(y_g + jnp.concatenate(diag, axis=1))
        new_state = jnp.dot(b_g.astype(F32).T.astype(BF16), xw2[:, gs], preferred_element_type=F32)
        state_ref[g] = prev * ea_x[L - 1:L, gs] + new_state

    y = jnp.concatenate(y_parts, axis=1) + xs * dskip_ref[...]
    y = y * z_ref[0].astype(F32)
    ng = ng_ref[...]
    outs = []
    for g in range(SSD_GROUPS):
        gs = slice(g * GROUP_WIDTH, (g + 1) * GROUP_WIDTH)
        outs.append(_rms_rows(y[:, gs], ng[:, gs]))
    o_ref[0] = jnp.concatenate(outs, axis=1).astype(o_ref.dtype)


def _ssd(xbc, z, dt_t, dtb, alog, dskip_x, ng, e3):
    b, s, _ = xbc.shape
    L = SSD_CHUNK
    const = lambda shape: pl.BlockSpec(shape, lambda i, j: (0,) * len(shape))
    return pl.pallas_call(
        _ssd_kernel,
        grid=(b, s // L),
        in_specs=[
            pl.BlockSpec((1, L, SSD_CONV_DIM), lambda i, j: (i, j, 0)),
            pl.BlockSpec((1, L, D_INNER), lambda i, j: (i, j, 0)),
            pl.BlockSpec((1, SSD_HEADS, L), lambda i, j: (i, 0, j)),
            const(dtb.shape), const(alog.shape), const(dskip_x.shape), const(ng.shape), const(e3.shape),
        ],
        out_specs=pl.BlockSpec((1, L, D_INNER), lambda i, j: (i, j, 0)),
        out_shape=jax.ShapeDtypeStruct((b, s, D_INNER), BF16),
        scratch_shapes=[pltpu.VMEM((SSD_GROUPS, SSD_STATE, GROUP_WIDTH), F32)],
        compiler_params=_params(("parallel", "arbitrary")),
        name="ssd",
    )(xbc, z, dt_t, dtb, alog, dskip_x, ng, e3)


def _attn_kernel(qt_ref, k_ref, vt_ref, o_ref, acc_ref, s_ref):
    nq = qt_ref.shape[2]
    heads = qt_ref.shape[1]
    T = ATTN_TILE
    row = lax.broadcasted_iota(jnp.int32, (T, T), 0)
    col = lax.broadcasted_iota(jnp.int32, (T, T), 1)
    diag_ok = row <= col

    def store_scores(qi, kj, slot):
        for h in range(heads):
            kt = k_ref[0, h, kj * T:(kj + 1) * T, :]
            s_ref[slot, h] = jnp.dot(kt, qt_ref[0, h, qi], preferred_element_type=F32)

    def softmax_update(h, kj, slot, m, masked):
        s = s_ref[slot, h]
        if masked:
            s = jnp.where(diag_ok, s, -jnp.inf)
        m_new = jnp.maximum(m, jnp.max(s, axis=0, keepdims=True))
        p = jnp.exp2(s - m_new).astype(BF16)
        pv = jnp.dot(vt_ref[0, h, kj], p, preferred_element_type=F32)
        acc_ref[h] = jnp.exp2(m - m_new) * acc_ref[h] + pv
        return m_new

    pairs = [(qi, kj) for qi in range(nq) for kj in range(qi + 1)]
    store_scores(0, 0, 0)
    carry = None
    for idx, (qi, kj) in enumerate(pairs):
        slot = idx % 2
        if kj == 0:
            acc_ref[...] = jnp.zeros_like(acc_ref)
            carry = tuple(jnp.full((1, T), -jnp.inf, F32) for _ in range(heads))
        if idx + 1 < len(pairs):
            store_scores(*pairs[idx + 1], 1 - slot)
        carry = tuple(softmax_update(h, kj, slot, carry[h], kj == qi) for h in range(heads))
        if kj == qi:
            outs = [acc_ref[h, :MLA_V_DIM] * (1.0 / acc_ref[h, MLA_V_DIM:MLA_V_DIM + 1]) for h in range(heads)]
            o = jnp.concatenate(outs, axis=0)
            o_ref[0, qi * T:(qi + 1) * T, :] = o.T.astype(o_ref.dtype)


def _attention(qt, k, vt):
    b, nh, nt, _, T = qt.shape
    s = nt * T
    hp = 2
    return pl.pallas_call(
        _attn_kernel,
        grid=(b, nh // hp),
        in_specs=[
            pl.BlockSpec((1, hp, nt, MLA_QK_PAD, T), lambda i, j: (i, j, 0, 0, 0)),
            pl.BlockSpec((1, hp, s, MLA_QK_PAD), lambda i, j: (i, j, 0, 0)),
            pl.BlockSpec((1, hp, nt, MLA_V_ROWS, T), lambda i, j: (i, j, 0, 0, 0)),
        ],
        out_specs=pl.BlockSpec((1, s, hp * MLA_V_DIM), lambda i, j: (i, 0, j)),
        out_shape=jax.ShapeDtypeStruct((b, s, nh * MLA_V_DIM), BF16),
        scratch_shapes=[pltpu.VMEM((hp, MLA_V_ROWS, T), F32), pltpu.VMEM((2, hp, T, T), F32)],
        compiler_params=_params(("parallel", "parallel")),
        name="attention",
    )(qt, k, vt)


def _mix_kernel(x_ref, ys_ref, at_ref, gt_ref, gb_ref, ws_ref, wm_ref, wo_ref, o_ref):
    y_ssd = jnp.dot(ys_ref[...], ws_ref[...], preferred_element_type=F32)
    y_mla = jnp.dot(at_ref[...], wm_ref[...], preferred_element_type=F32)
    g = _sigmoid(gt_ref[...].astype(F32) + gb_ref[...])
    mix = g[:, :D_MODEL] * y_ssd + g[:, D_MODEL:] * y_mla
    o_ref[...] = x_ref[...] + jnp.dot(mix.astype(BF16), wo_ref[...], preferred_element_type=F32)


def _mix(x2d, ys, at, gt, gb, ws, wm, wo, tm):
    t, d = x2d.shape
    tile = lambda n: pl.BlockSpec((tm, n), lambda i: (i, 0))
    const = lambda a: pl.BlockSpec(a.shape, lambda i: (0, 0))
    return pl.pallas_call(
        _mix_kernel,
        grid=(t // tm,),
        in_specs=[tile(d), tile(ys.shape[1]), tile(at.shape[1]), tile(gt.shape[1]),
                  const(gb), const(ws), const(wm), const(wo)],
        out_specs=tile(d),
        out_shape=jax.ShapeDtypeStruct((t, d), F32),
        compiler_params=_params(("parallel",)),
        name="mix",
    )(x2d, ys, at, gt, gb, ws, wm, wo)


def _ffn_kernel(x_ref, g_ref, wup_ref, cw_ref, cb_ref, wdn_ref, o_ref, tail_ref, perm_ref):
    nb, nt, d = x_ref.shape
    tm = nb * nt
    n_lane = d // LANES

    @pl.when(pl.program_id(1) == 0)
    def _():
        tail_ref[...] = jnp.zeros_like(tail_ref)

    x = _to_token_major(x_ref, perm_ref)
    h = _rms_rows(x, g_ref[...]).astype(BF16)

    keep = (FFN_CONV - 1) * nb
    rows = tm // FFN_ROW_SPLIT
    h_parts = [h[r * rows:(r + 1) * rows] for r in range(FFN_ROW_SPLIT)]

    def conv(us, c0):
        cs = slice(c0, c0 + FFN_COL_CHUNK)
        cw = cw_ref[:, cs]
        prev = tail_ref[:, cs]
        tail_ref[:, cs] = us[-1][rows - keep:]
        outs = []
        for u in us:
            ext = jnp.concatenate([prev, u], axis=0)
            out = cb_ref[:, cs] + u * cw[FFN_CONV - 1:FFN_CONV]
            for k in range(1, FFN_CONV):
                out = out + ext[keep - k * nb:keep - k * nb + rows] * cw[FFN_CONV - 1 - k:FFN_CONV - k]
            outs.append(out)
            prev = u[rows - keep:]
        return outs

    def up(col):
        return [jnp.dot(hp, wup_ref[:, col:col + FFN_COL_CHUNK], preferred_element_type=F32) for hp in h_parts]

    n_chunks = D_FF // FFN_COL_CHUNK
    accs = [x[r * rows:(r + 1) * rows] for r in range(FFN_ROW_SPLIT)]
    u_next = (up(0), up(D_FF))
    for c in range(n_chunks):
        c0 = c * FFN_COL_CHUNK
        u_gate, u_val = u_next
        if c + 1 < n_chunks:
            u_next = (up(c0 + FFN_COL_CHUNK), up(D_FF + c0 + FFN_COL_CHUNK))
        gate = conv(u_gate, c0)
        val = conv(u_val, D_FF + c0)
        for r in range(FFN_ROW_SPLIT):
            act = (_silu(gate[r]) * val[r]).astype(BF16)
            accs[r] = accs[r] + jnp.dot(act, wdn_ref[c0:c0 + FFN_COL_CHUNK, :], preferred_element_type=F32)
    acc = jnp.concatenate(accs, axis=0)

    _from_token_major(acc, o_ref, 0, perm_ref)


def _ffn(x, g, wup, cw, cb, wdn):
    b, s, d = x.shape
    nb, nt = FFN_BATCH, FFN_TOKENS
    const = lambda a: pl.BlockSpec(a.shape, lambda i, j: (0, 0), pipeline_mode=pl.Buffered(1))
    return pl.pallas_call(
        _ffn_kernel,
        grid=(b // nb, s // nt),
        in_specs=[pl.BlockSpec((nb, nt, d), lambda i, j: (i, j, 0)),
                  const(g), const(wup), const(cw), const(cb), const(wdn)],
        out_specs=pl.BlockSpec((nb, nt, d), lambda i, j: (i, j, 0)),
        out_shape=jax.ShapeDtypeStruct((b, s, d), F32),
        scratch_shapes=[pltpu.VMEM(((FFN_CONV - 1) * nb, 2 * D_FF), F32),
                        pltpu.VMEM((d // LANES, nb * nt, LANES), F32)],
        compiler_params=_params(("parallel", "arbitrary")),
        name="ffn",
    )(x, g, wup, cw, cb, wdn)


def _rope_tables_t(seq):
    inv = 1.0 / (ROPE_THETA ** (jnp.arange(0, MLA_ROPE, 2, dtype=F32) / MLA_ROPE))
    ang = inv[:, None] * jnp.arange(seq, dtype=F32)[None, :]
    return jnp.cos(ang), jnp.sin(ang)


def _head_expand_matrix():
    r = jnp.arange(4 * SSD_HEADS)[:, None]
    c = jnp.arange(D_INNER)[None, :]
    return ((r < 3 * SSD_HEADS) & ((r % SSD_HEADS) == (c // SSD_HEAD_DIM))).astype(BF16)


def _layer(x, p):
    b, s, d = x.shape
    t = b * s
    x2d = x.reshape(t, d)
    w_in = p["w_in"]
    col = lambda a: a.reshape(-1, 1)
    rowv = lambda a: a.reshape(1, -1)
    g_mix = rowv(p["norm_mix_g"])

    w_big = jnp.concatenate([w_in[:, :OFF_XBC], w_in[:, OFF_KR:]], axis=1).astype(BF16)
    z, xbc, gates = _norm_proj(x, g_mix, w_big, p["conv_ssd_w"], rowv(p["conv_ssd_b"]))

    ws_t = jnp.concatenate([w_in[:, OFF_DT:OFF_KVA], w_in[:, OFF_XBC:OFF_DT], w_in[:, OFF_KVA:OFF_KR]],
                           axis=1).T.astype(BF16)
    cos_t, sin_t = _rope_tables_t(s)
    qt, k, vt, dt_t = _mla_prep(
        x, g_mix, ws_t, col(p["q_a_norm_g"]), p["w_uq"].T.astype(BF16), col(p["kv_a_norm_g"]),
        p["w_ukv"].T.astype(BF16), col(p["q_norm_g"]), col(p["k_norm_g"]), cos_t, sin_t, min(512, s))

    y_ssd = _ssd(xbc, z, dt_t,
                 col(p["dt_bias"]), col(p["a_log"]), rowv(jnp.repeat(p["d_skip"], SSD_HEAD_DIM)),
                 rowv(p["ssd_norm_g"]), _head_expand_matrix())
    attn = _attention(qt, k, vt)

    x1 = _mix(x2d, y_ssd.reshape(t, -1), attn.reshape(t, -1), gates.reshape(t, -1), rowv(p["gate_b"]),
              p["w_ssd_proj"].astype(BF16), p["w_mla_proj"].astype(BF16), p["w_o"].astype(BF16), min(512, t))
    out = _ffn(x1.reshape(b, s, d), rowv(p["norm_ffn_g"]), p["w_up"].astype(BF16), p["conv_ffn_w"],
               rowv(p["conv_ffn_b"]), p["w_down"].astype(BF16))
    return out


def kernel(x, norm_mix_g, w_in, conv_ssd_w, conv_ssd_b, dt_bias, a_log, d_skip, ssd_norm_g, w_ssd_proj,
           q_a_norm_g, w_uq, kv_a_norm_g, w_ukv, q_norm_g, k_norm_g, w_mla_proj, gate_b, w_o, norm_ffn_g,
           w_up, conv_ffn_w, conv_ffn_b, w_down):
    params = dict(norm_mix_g=norm_mix_g, w_in=w_in, conv_ssd_w=conv_ssd_w, conv_ssd_b=conv_ssd_b,
                  dt_bias=dt_bias, a_log=a_log, d_skip=d_skip, ssd_norm_g=ssd_norm_g, w_ssd_proj=w_ssd_proj,
                  q_a_norm_g=q_a_norm_g, w_uq=w_uq, kv_a_norm_g=kv_a_norm_g, w_ukv=w_ukv, q_norm_g=q_norm_g,
                  k_norm_g=k_norm_g, w_mla_proj=w_mla_proj, gate_b=gate_b, w_o=w_o, norm_ffn_g=norm_ffn_g,
                  w_up=w_up, conv_ffn_w=conv_ffn_w, conv_ffn_b=conv_ffn_b, w_down=w_down)
    for i in range(w_in.shape[0]):
        x = _layer(x, {name: v[i] for name, v in params.items()})
    return x
```

```python
import functools

import jax
import jax.numpy as jnp
from jax import lax
from jax.experimental import pallas as pl
from jax.experimental.pallas import tpu as pltpu

F32 = jnp.float32
BF16 = jnp.bfloat16

D_MODEL = 1024
D_INNER = 2048
SSD_HEADS = 32
SSD_HEAD_DIM = 64
SSD_GROUPS = 4
SSD_STATE = 128
SSD_CONV = 4
SSD_CHUNK = 128
SSD_CHUNKS_PER_STEP = 4
SSD_CONV_DIM = D_INNER + 2 * SSD_GROUPS * SSD_STATE
GROUP_WIDTH = D_INNER // SSD_GROUPS
MLA_HEADS = 16
MLA_Q_LORA = 256
MLA_KV_LORA = 128
MLA_NOPE = 64
MLA_ROPE = 32
MLA_QK_DIM = MLA_NOPE + MLA_ROPE
MLA_QK_PAD = 128
MLA_V_DIM = 64
MLA_V_ROWS = 80
ROPE_THETA = 10000.0
D_FF = 2816
FFN_CONV = 3
NORM_EPS = 1e-6
LOG2_E = 1.4426950408889634

OFF_Z = D_INNER
OFF_XBC = OFF_Z + SSD_CONV_DIM
OFF_DT = OFF_XBC + SSD_HEADS
OFF_QA = OFF_DT + MLA_Q_LORA
OFF_KVA = OFF_QA + MLA_KV_LORA
OFF_KR = OFF_KVA + MLA_ROPE

SUBLANES = 8
LANES = 128
TOKMAJ_BATCH = SUBLANES
TOKMAJ_TOKENS = 64
FFN_BATCH = TOKMAJ_BATCH
FFN_TOKENS = TOKMAJ_TOKENS
PROJ_ROW_SPLIT = 2
FFN_ROW_SPLIT = 2
BF16_ROWS = 16
ATTN_TILE = 512
FFN_COL_CHUNK = 256
PROJ_COL_CHUNK = 256
VMEM_LIMIT = 56 * 1024 * 1024


def _params(sem, flags=None):
    return pltpu.CompilerParams(dimension_semantics=sem, vmem_limit_bytes=VMEM_LIMIT, flags=flags)


def _rms_rows(x, g):
    ms = jnp.mean(x * x, axis=-1, keepdims=True)
    return x * lax.rsqrt(ms + NORM_EPS) * g


def _rms_cols(x, g):
    ms = jnp.mean(x * x, axis=0, keepdims=True)
    return x * lax.rsqrt(ms + NORM_EPS) * g


def _sigmoid(x):
    return 1.0 / (1.0 + jnp.exp(-x))


def _silu(x):
    return x * _sigmoid(x)


def _to_token_major(x_ref, perm_ref):
    nb, nt, d = x_ref.shape
    for c in range(d // LANES):
        for j in range(nb):
            perm_ref[c, pl.ds(j, nt, stride=nb), :] = x_ref[j, :, c * LANES:(c + 1) * LANES]
    return jnp.concatenate([perm_ref[c] for c in range(d // LANES)], axis=1)


def _from_token_major(val, o_ref, c0, perm_ref):
    nb, nt, _ = o_ref.shape
    n_lane = val.shape[1] // LANES
    for c in range(n_lane):
        perm_ref[c] = val[:, c * LANES:(c + 1) * LANES]
    for c in range(n_lane):
        for j in range(nb):
            o_ref[j, :, c0 + c * LANES:c0 + (c + 1) * LANES] = perm_ref[c, pl.ds(j, nt, stride=nb), :].astype(o_ref.dtype)


def _norm_proj_kernel(x_ref, g_ref, w_ref, cw_ref, cb_ref, z_ref, xbc_ref, gates_ref, tail_ref, pin_ref, pout_ref):
    nb, nt, _ = x_ref.shape
    tm = nb * nt
    keep = (SSD_CONV - 1) * nb

    @pl.when(pl.program_id(1) == 0)
    def _():
        tail_ref[...] = jnp.zeros_like(tail_ref)

    g = g_ref[...]
    h_std = _rms_rows(x_ref[...].reshape(tm, x_ref.shape[2]), g).astype(BF16)

    def proj_std(c0):
        u = jnp.dot(h_std, w_ref[:, c0:c0 + PROJ_COL_CHUNK], preferred_element_type=F32)
        return u.reshape(nb, nt, PROJ_COL_CHUNK)

    def z_chunk(j):
        cs = slice(j * PROJ_COL_CHUNK, (j + 1) * PROJ_COL_CHUNK)
        z_ref[:, :, cs] = _silu(proj_std(cs.start)).astype(BF16)

    def gates_chunk(j):
        cs = slice(j * PROJ_COL_CHUNK, (j + 1) * PROJ_COL_CHUNK)
        gates_ref[:, :, cs] = proj_std(OFF_XBC + cs.start).astype(BF16)

    h = _rms_rows(_to_token_major(x_ref, pin_ref), g).astype(BF16)
    rows = tm // PROJ_ROW_SPLIT
    h_parts = [h[r * rows:(r + 1) * rows] for r in range(PROJ_ROW_SPLIT)]

    def proj(c0):
        return [jnp.dot(hp, w_ref[:, c0:c0 + PROJ_COL_CHUNK], preferred_element_type=F32) for hp in h_parts]

    def xbc_chunk(j):
        cs = slice(j * PROJ_COL_CHUNK, (j + 1) * PROJ_COL_CHUNK)
        us = proj(OFF_Z + cs.start)
        cw = cw_ref[:, cs]
        prev = tail_ref[:, cs]
        tail_ref[:, cs] = us[-1][rows - keep:]
        outs = []
        for u in us:
            ext = jnp.concatenate([prev, u], axis=0)
            acc = cb_ref[:, cs] + u * cw[SSD_CONV - 1:SSD_CONV]
            for k in range(1, SSD_CONV):
                acc = acc + ext[keep - k * nb:keep - k * nb + rows] * cw[SSD_CONV - 1 - k:SSD_CONV - k]
            outs.append(_silu(acc))
            prev = u[rows - keep:]
        _from_token_major(jnp.concatenate(outs, axis=0), xbc_ref, cs.start, pout_ref.at[j % 2])

    plain = [functools.partial(z_chunk, j) for j in range(D_INNER // PROJ_COL_CHUNK)]
    plain += [functools.partial(gates_chunk, j) for j in range(2 * D_MODEL // PROJ_COL_CHUNK)]
    n_x = SSD_CONV_DIM // PROJ_COL_CHUNK
    for j in range(max(n_x, len(plain))):
        if j < n_x:
            xbc_chunk(j)
        if j < len(plain):
            plain[j]()


def _norm_proj(x, g, w, cw, cb):
    b, s, d = x.shape
    nb, nt = TOKMAJ_BATCH, TOKMAJ_TOKENS
    widths = (D_INNER, SSD_CONV_DIM, 2 * D_MODEL)
    const = lambda a: pl.BlockSpec(a.shape, lambda i, j: (0, 0))
    return pl.pallas_call(
        _norm_proj_kernel,
        grid=(b // nb, s // nt),
        in_specs=[
            pl.BlockSpec((nb, nt, d), lambda i, j: (i, j, 0)),
            const(g),
            pl.BlockSpec(w.shape, lambda i, j: (0, 0), pipeline_mode=pl.Buffered(1)),
            const(cw), const(cb),
        ],
        out_specs=[pl.BlockSpec((nb, nt, n), lambda i, j: (i, j, 0)) for n in widths],
        out_shape=[jax.ShapeDtypeStruct((b, s, n), BF16) for n in widths],
        scratch_shapes=[pltpu.VMEM(((SSD_CONV - 1) * nb, SSD_CONV_DIM), F32),
                        pltpu.VMEM((d // LANES, nb * nt, LANES), F32),
                        pltpu.VMEM((2, PROJ_COL_CHUNK // LANES, nb * nt, LANES), F32)],
        compiler_params=_params(("parallel", "arbitrary")),
        name="norm_proj",
    )(x, g, w, cw, cb)


def _rope_rows(x1, x2, cos, sin):
    return x1 * cos - x2 * sin, x1 * sin + x2 * cos


def _mla_prep_kernel(x_ref, g_ref, ws_ref, qag_ref, wuq_ref, kvg_ref, wukv_ref, qg_ref, kg_ref,
                     cos_ref, sin_ref, qt_ref, k_ref, vt_ref, dt_ref):
    tm = x_ref.shape[1]
    h = _rms_rows(x_ref[0], g_ref[...]).astype(BF16)
    st = lax.dot_general(ws_ref[...], h, (((1,), (1,)), ((), ())), preferred_element_type=F32)
    qa = st[0:MLA_Q_LORA]
    kva = st[MLA_Q_LORA:MLA_Q_LORA + MLA_KV_LORA]
    o = MLA_Q_LORA + MLA_KV_LORA
    dt_ref[0] = st[o:o + SSD_HEADS]
    kr = st[o + SSD_HEADS:o + SSD_HEADS + MLA_ROPE]

    qa_n = _rms_cols(qa, qag_ref[...]).astype(BF16)
    kva_n = _rms_cols(kva, kvg_ref[...]).astype(BF16)
    q_all = jnp.dot(wuq_ref[...], qa_n, preferred_element_type=F32)
    kv_all = jnp.dot(wukv_ref[...], kva_n, preferred_element_type=F32)

    cos = cos_ref[...]
    sin = sin_ref[...]
    qg = qg_ref[...]
    kg = kg_ref[...]
    half = MLA_ROPE // 2
    scale = MLA_QK_DIM ** -0.5 * LOG2_E
    kr_ss = jnp.sum(kr * kr, axis=0, keepdims=True)
    pad = jnp.zeros((MLA_QK_PAD - MLA_QK_DIM, tm), F32)
    n_sub = tm // ATTN_TILE
    tail_row = lax.broadcasted_iota(jnp.int32, (MLA_V_ROWS - MLA_V_DIM, tm), 0)
    v_tail = (tail_row == 0).astype(F32)
    for hd in range(MLA_HEADS):
        qh = _rms_cols(q_all[MLA_QK_DIM * hd:MLA_QK_DIM * (hd + 1)], qg) * scale
        r1, r2 = _rope_rows(qh[MLA_NOPE:MLA_NOPE + half], qh[MLA_NOPE + half:], cos, sin)
        qfull = jnp.concatenate([qh[:MLA_NOPE], r1, r2, pad], axis=0).astype(BF16)

        base = (MLA_NOPE + MLA_V_DIM) * hd
        kn = kv_all[base:base + MLA_NOPE]
        v = jnp.concatenate([kv_all[base + MLA_NOPE:base + MLA_NOPE + MLA_V_DIM], v_tail], axis=0).astype(BF16)
        ss = (jnp.sum(kn * kn, axis=0, keepdims=True) + kr_ss) * (1.0 / MLA_QK_DIM)
        rs = lax.rsqrt(ss + NORM_EPS)
        kn = kn * rs * kg[:MLA_NOPE]
        krn = kr * rs * kg[MLA_NOPE:]
        r1, r2 = _rope_rows(krn[:half], krn[half:], cos, sin)
        kfull = jnp.concatenate([kn, r1, r2, pad], axis=0)
        k_ref[0, hd] = kfull.T.astype(BF16)
        for j in range(n_sub):
            sl = slice(j * ATTN_TILE, (j + 1) * ATTN_TILE)
            qt_ref[0, hd, j] = qfull[:, sl]
            vt_ref[0, hd, j] = v[:, sl]


def _mla_prep(x, g, ws_t, qag, wuq_t, kvg, wukv_t, qg, kg, cos_t, sin_t, tm):
    b, s, d = x.shape
    nt = s // ATTN_TILE
    n_sub = tm // ATTN_TILE
    const = lambda shape: pl.BlockSpec(shape, lambda i, j: (0,) * len(shape))
    return pl.pallas_call(
        _mla_prep_kernel,
        grid=(b, s // tm),
        in_specs=[
            pl.BlockSpec((1, tm, d), lambda i, j: (i, j, 0)),
            const(g.shape), const(ws_t.shape), const(qag.shape), const(wuq_t.shape),
            const(kvg.shape), const(wukv_t.shape), const(qg.shape), const(kg.shape),
            pl.BlockSpec((MLA_ROPE // 2, tm), lambda i, j: (0, j)),
            pl.BlockSpec((MLA_ROPE // 2, tm), lambda i, j: (0, j)),
        ],
        out_specs=[
            pl.BlockSpec((1, MLA_HEADS, n_sub, MLA_QK_PAD, ATTN_TILE), lambda i, j: (i, 0, j, 0, 0)),
            pl.BlockSpec((1, MLA_HEADS, tm, MLA_QK_PAD), lambda i, j: (i, 0, j, 0)),
            pl.BlockSpec((1, MLA_HEADS, n_sub, MLA_V_ROWS, ATTN_TILE), lambda i, j: (i, 0, j, 0, 0)),
            pl.BlockSpec((1, SSD_HEADS, tm), lambda i, j: (i, 0, j)),
        ],
        out_shape=[
            jax.ShapeDtypeStruct((b, MLA_HEADS, nt, MLA_QK_PAD, ATTN_TILE), BF16),
            jax.ShapeDtypeStruct((b, MLA_HEADS, s, MLA_QK_PAD), BF16),
            jax.ShapeDtypeStruct((b, MLA_HEADS, nt, MLA_V_ROWS, ATTN_TILE), BF16),
            jax.ShapeDtypeStruct((b, SSD_HEADS, s), F32),
        ],
        compiler_params=_params(("parallel", "parallel")),
        name="mla_prep",
    )(x, g, ws_t, qag, wuq_t, kvg, wukv_t, qg, kg, cos_t, sin_t)


def _split3_rows(v):
    hi = v.astype(BF16).astype(F32)
    r1 = v - hi
    mid = r1.astype(BF16).astype(F32)
    lo = (r1 - mid).astype(BF16).astype(F32)
    return jnp.concatenate([hi, mid, lo, jnp.zeros_like(v)], axis=0)


def _ssd_kernel(xbc_ref, z_ref, dt_ref, dtb_ref, alog_ref, dskip_ref, ng_ref, e3_ref, o_ref, state_ref):
    L = SSD_CHUNK

    @pl.when(pl.program_id(1) == 0)
    def _():
        state_ref[...] = jnp.zeros_like(state_ref)

    row = lax.broadcasted_iota(jnp.int32, (L, L), 0)
    col = lax.broadcasted_iota(jnp.int32, (L, L), 1)
    triu = (row <= col).astype(F32)
    causal = row >= col
    left = (col < SSD_HEAD_DIM).astype(BF16)
    right = (col >= SSD_HEAD_DIM).astype(BF16)
    heads_per_group = SSD_HEADS // SSD_GROUPS
    neg_a = -LOG2_E * jnp.exp(alog_ref[...])
    dtb = dtb_ref[...]
    dskip = dskip_ref[...]
    ng = ng_ref[...]

    def expand(v_t):
        p = _split3_rows(v_t).T.astype(BF16)
        return jnp.dot(p, e3_ref[...], preferred_element_type=F32)

    def chunk(c, states):
        rows = slice(c * L, (c + 1) * L)
        xs_b = xbc_ref[0, rows, :D_INNER]
        xs = xs_b.astype(F32)
        bm = xbc_ref[0, rows, D_INNER:D_INNER + SSD_GROUPS * SSD_STATE]
        cm = xbc_ref[0, rows, D_INNER + SSD_GROUPS * SSD_STATE:]

        dt_raw = dt_ref[0, :, rows] + dtb
        dt_t = jnp.maximum(dt_raw, 0.0) + jnp.log(1.0 + jnp.exp(-jnp.abs(dt_raw)))
        acum_t = jnp.dot(dt_t * neg_a, triu, preferred_element_type=F32, precision=lax.Precision.HIGHEST)
        w2_t = dt_t * jnp.exp2(acum_t[:, L - 1:L] - acum_t)
        ea_t = jnp.exp2(acum_t)
        src_t = acum_t - jnp.log2(dt_t)
        w2_x = expand(w2_t)
        ea_x = expand(ea_t)
        acum_c = jnp.concatenate([acum_t, jnp.zeros((L - SSD_HEADS, L), F32)], axis=0).T
        xw2 = (xs * w2_x).astype(BF16)

        y_parts, new_states = [], []
        for g in range(SSD_GROUPS):
            gs = slice(g * GROUP_WIDTH, (g + 1) * GROUP_WIDTH)
            b_g = bm[:, g * SSD_STATE:(g + 1) * SSD_STATE]
            c_g = cm[:, g * SSD_STATE:(g + 1) * SSD_STATE]
            scores = lax.dot_general(c_g, b_g, (((1,), (1,)), ((), ())), preferred_element_type=F32)
            prev = states[g]
            y_g = jnp.dot(c_g, prev.astype(BF16), preferred_element_type=F32) * ea_x[:, gs]
            diag = []
            for pair in range(heads_per_group // 2):
                ms = []
                for e in range(2):
                    hd = g * heads_per_group + 2 * pair + e
                    seg = acum_c[:, hd:hd + 1] - src_t[hd:hd + 1, :]
                    ms.append(scores * jnp.exp2(jnp.where(causal, seg, -jnp.inf)))
                m2 = jnp.concatenate(ms, axis=1).astype(BF16)
                c0 = g * GROUP_WIDTH + pair * 2 * SSD_HEAD_DIM
                x2 = xs_b[:, c0:c0 + 2 * SSD_HEAD_DIM]
                r2 = jnp.concatenate([x2 * left, x2 * right], axis=0)
                diag.append(jnp.dot(m2, r2, preferred_element_type=F32))
            y_parts.append(y_g + jnp.concatenate(diag, axis=1))
            new_state = jnp.dot(b_g.astype(F32).T.astype(BF16), xw2[:, gs], preferred_element_type=F32)
            new_states.append(prev * ea_x[L - 1:L, gs] + new_state)

        y = jnp.concatenate(y_parts, axis=1) + xs * dskip
        y = y * z_ref[0, rows, :].astype(F32)
        outs = []
        for g in range(SSD_GROUPS):
            gs = slice(g * GROUP_WIDTH, (g + 1) * GROUP_WIDTH)
            outs.append(_rms_rows(y[:, gs], ng[:, gs]))
        o_ref[0, rows, :] = jnp.concatenate(outs, axis=1).astype(o_ref.dtype)
        return new_states

    states = [state_ref[g] for g in range(SSD_GROUPS)]
    for c in range(xbc_ref.shape[1] // L):
        states = chunk(c, states)
    for g in range(SSD_GROUPS):
        state_ref[g] = states[g]


def _ssd(xbc, z, dt_t, dtb, alog, dskip_x, ng, e3):
    b, s, _ = xbc.shape
    L = SSD_CHUNK * SSD_CHUNKS_PER_STEP
    const = lambda shape: pl.BlockSpec(shape, lambda i, j: (0,) * len(shape))
    return pl.pallas_call(
        _ssd_kernel,
        grid=(b, s // L),
        in_specs=[
            pl.BlockSpec((1, L, SSD_CONV_DIM), lambda i, j: (i, j, 0)),
            pl.BlockSpec((1, L, D_INNER), lambda i, j: (i, j, 0)),
            pl.BlockSpec((1, SSD_HEADS, L), lambda i, j: (i, 0, j)),
            const(dtb.shape), const(alog.shape), const(dskip_x.shape), const(ng.shape), const(e3.shape),
        ],
        out_specs=pl.BlockSpec((1, L, D_INNER), lambda i, j: (i, j, 0)),
        out_shape=jax.ShapeDtypeStruct((b, s, D_INNER), BF16),
        scratch_shapes=[pltpu.VMEM((SSD_GROUPS, SSD_STATE, GROUP_WIDTH), F32)],
        compiler_params=_params(("parallel", "arbitrary")),
        name="ssd",
    )(xbc, z, dt_t, dtb, alog, dskip_x, ng, e3)


def _attn_kernel(qt_ref, k_ref, vt_ref, o_ref, acc_ref, s_ref):
    nq = qt_ref.shape[2]
    heads = qt_ref.shape[1]
    T = ATTN_TILE
    row = lax.broadcasted_iota(jnp.int32, (T, T), 0)
    col = lax.broadcasted_iota(jnp.int32, (T, T), 1)
    diag_ok = row <= col

    def store_scores(qi, kj, slot):
        for h in range(heads):
            kt = k_ref[0, h, kj * T:(kj + 1) * T, :]
            s_ref[slot, h] = jnp.dot(kt, qt_ref[0, h, qi], preferred_element_type=F32)

    def softmax_update(h, kj, slot, m, masked):
        s = s_ref[slot, h]
        if masked:
            s = jnp.where(diag_ok, s, -jnp.inf)
        m_new = jnp.maximum(m, jnp.max(s, axis=0, keepdims=True))
        p = jnp.exp2(s - m_new).astype(BF16)
        pv = jnp.dot(vt_ref[0, h, kj], p, preferred_element_type=F32)
        acc_ref[h] = jnp.exp2(m - m_new) * acc_ref[h] + pv
        return m_new

    pairs = [(qi, kj) for qi in range(nq) for kj in range(qi + 1)]
    store_scores(0, 0, 0)
    carry = None
    for idx, (qi, kj) in enumerate(pairs):
        slot = idx % 2
        if kj == 0:
            acc_ref[...] = jnp.zeros_like(acc_ref)
            carry = tuple(jnp.full((1, T), -jnp.inf, F32) for _ in range(heads))
        if idx + 1 < len(pairs):
            store_scores(*pairs[idx + 1], 1 - slot)
        carry = tuple(softmax_update(h, kj, slot, carry[h], kj == qi) for h in range(heads))
        if kj == qi:
            outs = [acc_ref[h, :MLA_V_DIM] * (1.0 / acc_ref[h, MLA_V_DIM:MLA_V_DIM + 1]) for h in range(heads)]
            o = jnp.concatenate(outs, axis=0)
            o_ref[0, qi * T:(qi + 1) * T, :] = o.T.astype(o_ref.dtype)


def _attention(qt, k, vt):
    b, nh, nt, _, T = qt.shape
    s = nt * T
    hp = 2
    return pl.pallas_call(
        _attn_kernel,
        grid=(b, nh // hp),
        in_specs=[
            pl.BlockSpec((1, hp, nt, MLA_QK_PAD, T), lambda i, j: (i, j, 0, 0, 0)),
            pl.BlockSpec((1, hp, s, MLA_QK_PAD), lambda i, j: (i, j, 0, 0)),
            pl.BlockSpec((1, hp, nt, MLA_V_ROWS, T), lambda i, j: (i, j, 0, 0, 0)),
        ],
        out_specs=pl.BlockSpec((1, s, hp * MLA_V_DIM), lambda i, j: (i, 0, j)),
        out_shape=jax.ShapeDtypeStruct((b, s, nh * MLA_V_DIM), BF16),
        scratch_shapes=[pltpu.VMEM((hp, MLA_V_ROWS, T), F32), pltpu.VMEM((2, hp, T, T), F32)],
        compiler_params=_params(("parallel", "parallel")),
        name="attention",
    )(qt, k, vt)


def _mix_kernel(x_ref, ys_ref, at_ref, gt_ref, gb_ref, ws_ref, wm_ref, wo_ref, o_ref):
    y_ssd = jnp.dot(ys_ref[...], ws_ref[...], preferred_element_type=F32)
    y_mla = jnp.dot(at_ref[...], wm_ref[...], preferred_element_type=F32)
    g = _sigmoid(gt_ref[...].astype(F32) + gb_ref[...])
    mix = g[:, :D_MODEL] * y_ssd + g[:, D_MODEL:] * y_mla
    o_ref[...] = x_ref[...] + jnp.dot(mix.astype(BF16), wo_ref[...], preferred_element_type=F32)


def _mix(x2d, ys, at, gt, gb, ws, wm, wo, tm):
    t, d = x2d.shape
    tile = lambda n: pl.BlockSpec((tm, n), lambda i: (i, 0))
    const = lambda a: pl.BlockSpec(a.shape, lambda i: (0, 0))
    return pl.pallas_call(
        _mix_kernel,
        grid=(t // tm,),
        in_specs=[tile(d), tile(ys.shape[1]), tile(at.shape[1]), tile(gt.shape[1]),
                  const(gb), const(ws), const(wm), const(wo)],
        out_specs=tile(d),
        out_shape=jax.ShapeDtypeStruct((t, d), F32),
        compiler_params=_params(("parallel",)),
        name="mix",
    )(x2d, ys, at, gt, gb, ws, wm, wo)


def _ffn_kernel(x_ref, g_ref, wup_ref, cw_ref, cb_ref, wdn_ref, o_ref, tail_ref, perm_ref):
    nb, nt, d = x_ref.shape
    tm = nb * nt
    n_lane = d // LANES

    @pl.when(pl.program_id(1) == 0)
    def _():
        tail_ref[...] = jnp.zeros_like(tail_ref)

    x = _to_token_major(x_ref, perm_ref)
    h = _rms_rows(x, g_ref[...]).astype(BF16)

    keep = (FFN_CONV - 1) * nb
    rows = tm // FFN_ROW_SPLIT
    h_parts = [h[r * rows:(r + 1) * rows] for r in range(FFN_ROW_SPLIT)]

    def conv(us, c0):
        cs = slice(c0, c0 + FFN_COL_CHUNK)
        cw = cw_ref[:, cs]
        prev = tail_ref[:, cs]
        tail_ref[:, cs] = us[-1][rows - keep:]
        outs = []
        for u in us:
            ext = jnp.concatenate([prev, u], axis=0)
            out = cb_ref[:, cs] + u * cw[FFN_CONV - 1:FFN_CONV]
            for k in range(1, FFN_CONV):
                out = out + ext[keep - k * nb:keep - k * nb + rows] * cw[FFN_CONV - 1 - k:FFN_CONV - k]
            outs.append(out)
            prev = u[rows - keep:]
        return outs

    def up(col):
        return [jnp.dot(hp, wup_ref[:, col:col + FFN_COL_CHUNK], preferred_element_type=F32) for hp in h_parts]

    n_chunks = D_FF // FFN_COL_CHUNK
    accs = [x[r * rows:(r + 1) * rows] for r in range(FFN_ROW_SPLIT)]
    u_next = (up(0), up(D_FF))
    for c in range(n_chunks):
        c0 = c * FFN_COL_CHUNK
        u_gate, u_val = u_next
        if c + 1 < n_chunks:
            u_next = (up(c0 + FFN_COL_CHUNK), up(D_FF + c0 + FFN_COL_CHUNK))
        gate = conv(u_gate, c0)
        val = conv(u_val, D_FF + c0)
        for r in range(FFN_ROW_SPLIT):
            act = (_silu(gate[r]) * val[r]).astype(BF16)
            accs[r] = accs[r] + jnp.dot(act, wdn_ref[c0:c0 + FFN_COL_CHUNK, :], preferred_element_type=F32)
    acc = jnp.concatenate(accs, axis=0)

    _from_token_major(acc, o_ref, 0, perm_ref)


def _ffn(x, g, wup, cw, cb, wdn):
    b, s, d = x.shape
    nb, nt = FFN_BATCH, FFN_TOKENS
    const = lambda a: pl.BlockSpec(a.shape, lambda i, j: (0, 0), pipeline_mode=pl.Buffered(1))
    return pl.pallas_call(
        _ffn_kernel,
        grid=(b // nb, s // nt),
        in_specs=[pl.BlockSpec((nb, nt, d), lambda i, j: (i, j, 0)),
                  const(g), const(wup), const(cw), const(cb), const(wdn)],
        out_specs=pl.BlockSpec((nb, nt, d), lambda i, j: (i, j, 0)),
        out_shape=jax.ShapeDtypeStruct((b, s, d), F32),
        scratch_shapes=[pltpu.VMEM(((FFN_CONV - 1) * nb, 2 * D_FF), F32),
                        pltpu.VMEM((d // LANES, nb * nt, LANES), F32)],
        compiler_params=_params(("parallel", "arbitrary")),
        name="ffn",
    )(x, g, wup, cw, cb, wdn)


def _rope_tables_t(seq):
    inv = 1.0 / (ROPE_THETA ** (jnp.arange(0, MLA_ROPE, 2, dtype=F32) / MLA_ROPE))
    ang = inv[:, None] * jnp.arange(seq, dtype=F32)[None, :]
    return jnp.cos(ang), jnp.sin(ang)


def _head_expand_matrix():
    r = jnp.arange(4 * SSD_HEADS)[:, None]
    c = jnp.arange(D_INNER)[None, :]
    return ((r < 3 * SSD_HEADS) & ((r % SSD_HEADS) == (c // SSD_HEAD_DIM))).astype(BF16)


def _layer(x, p):
    b, s, d = x.shape
    t = b * s
    x2d = x.reshape(t, d)
    w_in = p["w_in"]
    col = lambda a: a.reshape(-1, 1)
    rowv = lambda a: a.reshape(1, -1)
    g_mix = rowv(p["norm_mix_g"])

    w_big = jnp.concatenate([w_in[:, :OFF_XBC], w_in[:, OFF_KR:]], axis=1).astype(BF16)
    z, xbc, gates = _norm_proj(x, g_mix, w_big, p["conv_ssd_w"], rowv(p["conv_ssd_b"]))

    ws_t = jnp.concatenate([w_in[:, OFF_DT:OFF_KVA], w_in[:, OFF_XBC:OFF_DT], w_in[:, OFF_KVA:OFF_KR]],
                           axis=1).T.astype(BF16)
    cos_t, sin_t = _rope_tables_t(s)
    qt, k, vt, dt_t = _mla_prep(
        x, g_mix, ws_t, col(p["q_a_norm_g"]), p["w_uq"].T.astype(BF16), col(p["kv_a_norm_g"]),
        p["w_ukv"].T.astype(BF16), col(p["q_norm_g"]), col(p["k_norm_g"]), cos_t, sin_t, min(512, s))

    y_ssd = _ssd(xbc, z, dt_t,
                 col(p["dt_bias"]), col(p["a_log"]), rowv(jnp.repeat(p["d_skip"], SSD_HEAD_DIM)),
                 rowv(p["ssd_norm_g"]), _head_expand_matrix())
    attn = _attention(qt, k, vt)

    x1 = _mix(x2d, y_ssd.reshape(t, -1), attn.reshape(t, -1), gates.reshape(t, -1), rowv(p["gate_b"]),
              p["w_ssd_proj"].astype(BF16), p["w_mla_proj"].astype(BF16), p["w_o"].astype(BF16), min(512, t))
    out = _ffn(x1.reshape(b, s, d), rowv(p["norm_ffn_g"]), p["w_up"].astype(BF16), p["conv_ffn_w"],
               rowv(p["conv_ffn_b"]), p["w_down"].astype(BF16))
    return out


def kernel(x, norm_mix_g, w_in, conv_ssd_w, conv_ssd_b, dt_bias, a_log, d_skip, ssd_norm_g, w_ssd_proj,
           q_a_norm_g, w_uq, kv_a_norm_g, w_ukv, q_norm_g, k_norm_g, w_mla_proj, gate_b, w_o, norm_ffn_g,
           w_up, conv_ffn_w, conv_ffn_b, w_down):
    params = dict(norm_mix_g=norm_mix_g, w_in=w_in, conv_ssd_w=conv_ssd_w, conv_ssd_b=conv_ssd_b,
                  dt_bias=dt_bias, a_log=a_log, d_skip=d_skip, ssd_norm_g=ssd_norm_g, w_ssd_proj=w_ssd_proj,
                  q_a_norm_g=q_a_norm_g, w_uq=w_uq, kv_a_norm_g=kv_a_norm_g, w_ukv=w_ukv, q_norm_g=q_norm_g,
                  k_norm_g=k_norm_g, w_mla_proj=w_mla_proj, gate_b=gate_b, w_o=w_o, norm_ffn_g=norm_ffn_g,
                  w_up=w_up, conv_ffn_w=conv_ffn_w, conv_ffn_b=conv_ffn_b, w_down=w_down)
    for i in range(w_in.shape[0]):
        x = _layer(x, {name: v[i] for name, v in params.items()})
    return x
```
